```python
import math
import jax, jax.numpy as jnp
from jax import lax
import numpy as np

D_MODEL = 1024
BATCH = 16
SEQ = 4096
DEPTH = 4

CHUNK = 64
N_MIXERS = 4
Q_BLOCK = 128
EPS = 1e-6
GMLP_CHUNK = 128
GMLP_WIDTH = 1024
GMLP_GROUPS = 8
GMLP_GROUP_DIM = GMLP_WIDTH // GMLP_GROUPS
DIFF_HEADS = 8
DIFF_HEAD_DIM = 64
DIFF_V_DIM = 2 * DIFF_HEAD_DIM
FOX_HEADS = 16
FOX_HEAD_DIM = 64
RNN_WIDTH = 1280
RNN_BLOCKS = 16
RNN_BLOCK_DIM = RNN_WIDTH // RNN_BLOCKS
RNN_CONV = 4
RGLRU_C = 8.0
D_FF = 2816
FFN_CONV = 3
REL_BUCKETS = 32
REL_MAX_DIST = 128
N_A = (DEPTH - 0 + N_MIXERS - 1) // N_MIXERS
N_B = (DEPTH - 1 + N_MIXERS - 1) // N_MIXERS
N_C = (DEPTH - 2 + N_MIXERS - 1) // N_MIXERS
N_D = (DEPTH - 3 + N_MIXERS - 1) // N_MIXERS

kernel_name = "hybrid_chunk_causal_interleaved_trunk"


def rmsnorm(x, g):
    xf = x.astype(jnp.float32)
    y = xf * lax.rsqrt(jnp.mean(xf * xf, axis=-1, keepdims=True) + EPS)
    return (y * g.astype(jnp.float32)).astype(x.dtype)


def layernorm(x, g, b):
    xf = x.astype(jnp.float32)
    mu = jnp.mean(xf, axis=-1, keepdims=True)
    var = jnp.mean(jnp.square(xf - mu), axis=-1, keepdims=True)
    y = (xf - mu) * lax.rsqrt(var + EPS)
    return (y * g.astype(jnp.float32) + b.astype(jnp.float32)).astype(x.dtype)


def causal_dwconv(x, w, b):
    K, C = w.shape
    y = lax.conv_general_dilated(x, w[:, None, :].astype(x.dtype), window_strides=(1,),
                                 padding=[(K - 1, 0)], dimension_numbers=('NWC', 'WIO', 'NWC'),
                                 feature_group_count=C)
    return y + b.astype(x.dtype)


def chunk_mask(qpos, kpos):
    return (kpos[None, :] // CHUNK) <= (qpos[:, None] // CHUNK)


def t5_bucket(rel):
    half = REL_BUCKETS // 2
    max_exact = half // 2
    n = jnp.abs(rel)
    ret = jnp.where(rel > 0, half, 0)
    nf = jnp.maximum(n, 1).astype(jnp.float32)
    large = max_exact + (jnp.log(nf / max_exact) / math.log(REL_MAX_DIST / max_exact)
                         * (half - max_exact)).astype(jnp.int32)
    large = jnp.minimum(large, half - 1)
    return ret + jnp.where(n < max_exact, n, large)


def gmlp_mixer(x, w_in, ln_g, ln_b, w_s, b_s, w_out):
    B, S, _ = x.shape
    z = jax.nn.gelu(x @ w_in)
    u, v = z[..., :GMLP_WIDTH], z[..., GMLP_WIDTH:]
    v = layernorm(v, ln_g, ln_b)
    v = v.reshape(B, S // GMLP_CHUNK, GMLP_CHUNK, GMLP_GROUPS, GMLP_GROUP_DIM)
    p = jnp.arange(GMLP_CHUNK)
    mask = (p[None, :] // CHUNK) <= (p[:, None] // CHUNK)
    ws = jnp.where(mask[None], w_s, jnp.zeros((), w_s.dtype))
    v = jnp.einsum('gpq,bnqgc->bnpgc', ws, v) + b_s.T[None, None, :, :, None]
    return (u * v.reshape(B, S, GMLP_WIDTH)) @ w_out


def diff_attention(x, w_in, lam, sub_g, w_out, rel_bias, lambda_init):
    B, S, _ = x.shape
    H, d = DIFF_HEADS, DIFF_HEAD_DIM
    hq = H * 2 * d
    qkv = x @ w_in
    q = qkv[..., :hq].reshape(B, S, H, 2, d).transpose(0, 2, 1, 3, 4)
    k = qkv[..., hq:2 * hq].reshape(B, S, H, 2, d).transpose(0, 2, 1, 3, 4)
    v = qkv[..., 2 * hq:].reshape(B, S, H, DIFF_V_DIM).transpose(0, 2, 1, 3)
    lamf = lam.astype(jnp.float32)
    lam_full = jnp.exp(jnp.sum(lamf[0] * lamf[1])) - jnp.exp(jnp.sum(lamf[2] * lamf[3])) + lambda_init
    scale = d ** -0.5
    outs = []
    for q0 in range(0, S, Q_BLOCK):
        end = q0 + Q_BLOCK
        qpos = q0 + jnp.arange(Q_BLOCK)
        kpos = jnp.arange(end)
        bias = rel_bias[t5_bucket(kpos[None, :] - qpos[:, None])].astype(jnp.float32).transpose(2, 0, 1)
        s = jnp.einsum('bhqid,bhkid->ibhqk', q[:, :, q0:end], k[:, :, :end]).astype(jnp.float32) * scale + bias
        s = jnp.where(chunk_mask(qpos, kpos), s, -jnp.inf)
        p = jax.nn.softmax(s, axis=-1)
        attn = (p[0] - lam_full * p[1]).astype(v.dtype)
        outs.append(jnp.einsum('bhqk,bhkd->bhqd', attn, v[:, :, :end]))
    o = jnp.concatenate(outs, axis=2)
    o = rmsnorm(o, sub_g) * (1.0 - lambda_init)
    return o.transpose(0, 2, 1, 3).reshape(B, S, H * DIFF_V_DIM) @ w_out


def forgetting_attention(x, w_in, b_f, w_out):
    B, S, _ = x.shape
    H, d = FOX_HEADS, FOX_HEAD_DIM
    hd = H * d
    proj = x @ w_in
    q = proj[..., :hd].reshape(B, S, H, d).transpose(0, 2, 1, 3)
    k = proj[..., hd:2 * hd].reshape(B, S, H, d).transpose(0, 2, 1, 3)
    v = proj[..., 2 * hd:3 * hd].reshape(B, S, H, d).transpose(0, 2, 1, 3)
    f_logit = proj[..., 3 * hd:].astype(jnp.float32) + b_f.astype(jnp.float32)
    cum = jnp.cumsum(jax.nn.log_sigmoid(f_logit), axis=1).transpose(0, 2, 1)
    scale = d ** -0.5
    outs = []
    for q0 in range(0, S, Q_BLOCK):
        end = q0 + Q_BLOCK
        qpos = q0 + jnp.arange(Q_BLOCK)
        kpos = jnp.arange(end)
        decay = cum[:, :, q0:end, None] - cum[:, :, None, :end]
        s = jnp.einsum('bhqd,bhkd->bhqk', q[:, :, q0:end], k[:, :, :end]).astype(jnp.float32) * scale + decay
        s = jnp.where(kpos[None, :] <= qpos[:, None], s, -jnp.inf)
        p = jax.nn.softmax(s, axis=-1).astype(v.dtype)
        outs.append(jnp.einsum('bhqk,bhkd->bhqd', p, v[:, :, :end]))
    o = jnp.concatenate(outs, axis=2)
    return o.transpose(0, 2, 1, 3).reshape(B, S, hd) @ w_out


def _linear_recurrence_combine(left, right):
    a1, b1 = left
    a2, b2 = right
    return (a1 * a2, a2 * b1 + b2)


def rglru_block(x, w_in, conv_w, conv_b, w_r, b_r, w_i, b_i, lam, w_out):
    B, S, _ = x.shape
    z = x @ w_in
    gate, xr = z[..., :RNN_WIDTH], z[..., RNN_WIDTH:]
    xr = causal_dwconv(xr, conv_w, conv_b).astype(jnp.float32)
    xb = xr.reshape(B, S, RNN_BLOCKS, RNN_BLOCK_DIM)
    r = jax.nn.sigmoid(jnp.einsum('bsnc,ncd->bsnd', xb, w_r.astype(jnp.float32)).reshape(B, S, RNN_WIDTH)
                       + b_r.astype(jnp.float32))
    i = jax.nn.sigmoid(jnp.einsum('bsnc,ncd->bsnd', xb, w_i.astype(jnp.float32)).reshape(B, S, RNN_WIDTH)
                       + b_i.astype(jnp.float32))
    log_a = -RGLRU_C * r * jax.nn.softplus(-lam.astype(jnp.float32))
    a = jnp.exp(log_a)
    u = jnp.sqrt(-jnp.expm1(2.0 * log_a)) * (i * xr)
    _, h = lax.associative_scan(_linear_recurrence_combine, (a, u), axis=1)
    y = h.astype(x.dtype) * jax.nn.gelu(gate)
    return y @ w_out


def conv_ffn(x, w_up, conv_w, conv_b, w_down):
    h = causal_dwconv(x @ w_up, conv_w, conv_b)
    g, u = h[..., :D_FF], h[..., D_FF:]
    return (jax.nn.gelu(g) * u) @ w_down


def setup_inputs(seed: int = 0) -> dict:
    key = jax.random.key(seed)
    ks = jax.random.split(key, 32)
    f32 = jnp.float32

    def nrm(i, shape, scale):
        return scale * jax.random.normal(ks[i], shape, f32)

    D = D_MODEL
    a8 = jax.random.uniform(ks[28], (N_D, RNN_WIDTH), f32, 0.9, 0.999)
    a_base = a8 ** (1.0 / RGLRU_C)
    return {
        "x": nrm(0, (BATCH, SEQ, D), 1.0),
        "norm_g": 1.0 + nrm(1, (DEPTH, 4, D), 0.1),
        "ffn_w_up": nrm(2, (DEPTH, D, 2 * D_FF), D ** -0.5),
        "ffn_conv_w": nrm(3, (DEPTH, FFN_CONV, 2 * D_FF), FFN_CONV ** -0.5),
        "ffn_conv_b": nrm(4, (DEPTH, 2 * D_FF), 0.01),
        "ffn_w_down": nrm(5, (DEPTH, D_FF, D), D_FF ** -0.5),
        "rel_bias": nrm(6, (REL_BUCKETS, DIFF_HEADS), 0.5),
        "a_w_in": nrm(7, (N_A, D, 2 * GMLP_WIDTH), D ** -0.5),
        "a_ln_g": 1.0 + nrm(8, (N_A, GMLP_WIDTH), 0.1),
        "a_ln_b": nrm(9, (N_A, GMLP_WIDTH), 0.05),
        "a_w_s": nrm(10, (N_A, GMLP_GROUPS, GMLP_CHUNK, GMLP_CHUNK), GMLP_CHUNK ** -0.5),
        "a_b_s": 1.0 + nrm(11, (N_A, GMLP_GROUPS, GMLP_CHUNK), 0.1),
        "a_w_out": nrm(12, (N_A, GMLP_WIDTH, D), GMLP_WIDTH ** -0.5),
        "b_w_in": nrm(13, (N_B, D, 3 * DIFF_HEADS * DIFF_V_DIM), D ** -0.5),
        "b_lam": nrm(14, (N_B, 4, DIFF_HEAD_DIM), 0.1),
        "b_sub_g": 1.0 + nrm(15, (N_B, DIFF_V_DIM), 0.1),
        "b_w_out": nrm(16, (N_B, DIFF_HEADS * DIFF_V_DIM, D), (DIFF_HEADS * DIFF_V_DIM) ** -0.5),
        "c_w_in": nrm(17, (N_C, D, 3 * FOX_HEADS * FOX_HEAD_DIM + FOX_HEADS), D ** -0.5),
        "c_b_f": jax.random.uniform(ks[18], (N_C, FOX_HEADS), f32, 3.0, 6.0),
        "c_w_out": nrm(19, (N_C, FOX_HEADS * FOX_HEAD_DIM, D), (FOX_HEADS * FOX_HEAD_DIM) ** -0.5),
        "d_w_in": nrm(20, (N_D, D, 2 * RNN_WIDTH), D ** -0.5),
        "d_conv_w": nrm(21, (N_D, RNN_CONV, RNN_WIDTH), RNN_CONV ** -0.5),
        "d_conv_b": nrm(22, (N_D, RNN_WIDTH), 0.01),
        "d_w_r": nrm(23, (N_D, RNN_BLOCKS, RNN_BLOCK_DIM, RNN_BLOCK_DIM), RNN_BLOCK_DIM ** -0.5),
        "d_b_r": nrm(24, (N_D, RNN_WIDTH), 0.05),
        "d_w_i": nrm(25, (N_D, RNN_BLOCKS, RNN_BLOCK_DIM, RNN_BLOCK_DIM), RNN_BLOCK_DIM ** -0.5),
        "d_b_i": nrm(26, (N_D, RNN_WIDTH), 0.05),
        "d_lam": jnp.log(a_base) - jnp.log1p(-a_base),
        "d_w_out": nrm(27, (N_D, RNN_WIDTH, D), RNN_WIDTH ** -0.5),
    }


def reference(x, norm_g, ffn_w_up, ffn_conv_w, ffn_conv_b, ffn_w_down, rel_bias,
              a_w_in, a_ln_g, a_ln_b, a_w_s, a_b_s, a_w_out,
              b_w_in, b_lam, b_sub_g, b_w_out,
              c_w_in, c_b_f, c_w_out,
              d_w_in, d_conv_w, d_conv_b, d_w_r, d_b_r, d_w_i, d_b_i, d_lam, d_w_out):
    h = x
    for layer in range(DEPTH):
        m = layer % N_MIXERS
        j = layer // N_MIXERS
        y = rmsnorm(h, norm_g[layer, 0])
        if m == 0:
            y = gmlp_mixer(y, a_w_in[j], a_ln_g[j], a_ln_b[j], a_w_s[j], a_b_s[j], a_w_out[j])
        elif m == 1:
            lambda_init = 0.8 - 0.6 * math.exp(-0.3 * layer)
            y = diff_attention(y, b_w_in[j], b_lam[j], b_sub_g[j], b_w_out[j], rel_bias, lambda_init)
        elif m == 2:
            y = forgetting_attention(y, c_w_in[j], c_b_f[j], c_w_out[j])
        else:
            y = rglru_block(y, d_w_in[j], d_conv_w[j], d_conv_b[j], d_w_r[j], d_b_r[j],
                            d_w_i[j], d_b_i[j], d_lam[j], d_w_out[j])
        h = h + rmsnorm(y, norm_g[layer, 1])
        y = rmsnorm(h, norm_g[layer, 2])
        y = conv_ffn(y, ffn_w_up[layer], ffn_conv_w[layer], ffn_conv_b[layer], ffn_w_down[layer])
        h = h + rmsnorm(y, norm_g[layer, 3])
    return h
```

```python
import functools
import math

import jax
import jax.numpy as jnp
from jax import lax
from jax.experimental import pallas as pl
from jax.experimental.pallas import tpu as pltpu

F32 = jnp.float32
BF16 = jnp.bfloat16

D_MODEL = 1024
CHUNK = 64
N_MIXERS = 4
EPS = 1e-6
GMLP_CHUNK = 128
GMLP_WIDTH = 1024
GMLP_GROUPS = 8
DIFF_HEADS = 8
DIFF_HEAD_DIM = 64
FOX_HEADS = 16
FOX_HEAD_DIM = 64
RNN_WIDTH = 1280
RNN_BLOCKS = 16
RNN_BLOCK_DIM = RNN_WIDTH // RNN_BLOCKS
RNN_CONV = 4
RGLRU_C = 8.0
D_FF = 2816
FFN_CONV = 3
REL_BUCKETS = 32
REL_MAX_DIST = 128

LANES = 128
SUBLANES = 8
MASKED = -1e30
VMEM_LIMIT = 56 * 1024 * 1024

FFN_TILE = 512
FFN_CHUNK = 256
ROW_TILE = 512
GMLP_TILE = 256
RNN_TILE = 256
ATT_TILE = 256


def _params(*sem):
    return pltpu.CompilerParams(dimension_semantics=sem, vmem_limit_bytes=VMEM_LIMIT)


def _resident(shape):
    zeros = (0,) * len(shape)
    return pl.BlockSpec(shape, lambda *_: zeros, pipeline_mode=pl.Buffered(1))


def _rows(tm, width):
    return pl.BlockSpec((None, tm, width), lambda b, s: (b, s, 0))


def _rms(x, g):
    return x * lax.rsqrt(jnp.mean(x * x, axis=-1, keepdims=True) + EPS) * g


def _dot(a, b):
    return jnp.dot(a, b, preferred_element_type=F32)


def _ffn_kernel(h_ref, gin_ref, wup_ref, cw_ref, cb_ref, wdn_ref, gout_ref, o_ref,
                carry_ref, ext_ref, acc_ref, y_ref, *, tm, nc):
    @pl.when(pl.program_id(1) == 0)
    def _():
        carry_ref[...] = jnp.zeros_like(carry_ref)

    h = h_ref[...]
    y_ref[...] = _rms(h, gin_ref[...]).astype(BF16)
    acc_ref[...] = jnp.zeros_like(acc_ref)

    def conv(idx, slot):
        up = _dot(y_ref[...], wup_ref[idx])
        ext_ref[slot, 0:SUBLANES, :] = carry_ref[idx]
        ext_ref[slot, SUBLANES:SUBLANES + tm, :] = up
        carry_ref[idx] = up[tm - SUBLANES:tm, :]
        w = cw_ref[idx]
        return (up * w[2:3] + ext_ref[slot, pl.ds(SUBLANES - 1, tm), :] * w[1:2]
                + ext_ref[slot, pl.ds(SUBLANES - 2, tm), :] * w[0:1] + cb_ref[idx])

    def body(c, carry):
        cg = conv(c, 0)
        cu = conv(c + nc, 1)
        act = (jax.nn.gelu(cg) * cu).astype(BF16)
        acc_ref[...] += _dot(act, wdn_ref[c])
        return carry

    lax.fori_loop(0, nc, body, 0)
    o_ref[...] = h + _rms(acc_ref[...], gout_ref[...])


def _conv_ffn(h, g_in, w_up, conv_w, conv_b, w_down, g_out):
    B, S, D = h.shape
    tm, fc = min(FFN_TILE, S), FFN_CHUNK
    nc = D_FF // fc
    assert D_FF % fc == 0 and S % tm == 0
    wup = w_up.astype(BF16).reshape(D, 2 * nc, fc).transpose(1, 0, 2)
    cw = conv_w.reshape(FFN_CONV, 2 * nc, fc).transpose(1, 0, 2)
    cb = conv_b.reshape(2 * nc, 1, fc)
    wdn = w_down.astype(BF16).reshape(nc, fc, D)
    return pl.pallas_call(
        functools.partial(_ffn_kernel, tm=tm, nc=nc),
        out_shape=jax.ShapeDtypeStruct((B, S, D), F32),
        grid=(B, S // tm),
        in_specs=[_rows(tm, D), _resident((1, D)), _resident(wup.shape), _resident(cw.shape),
                  _resident(cb.shape), _resident(wdn.shape), _resident((1, D))],
        out_specs=_rows(tm, D),
        scratch_shapes=[pltpu.VMEM((2 * nc, SUBLANES, fc), F32),
                        pltpu.VMEM((2, SUBLANES + tm, fc), F32),
                        pltpu.VMEM((tm, D), F32),
                        pltpu.VMEM((tm, D), BF16)],
        compiler_params=_params("parallel", "arbitrary"),
        name="conv_ffn",
    )(h, g_in.reshape(1, D), wup, cw, cb, wdn, g_out.reshape(1, D))


def _gmlp_kernel(h_ref, gin_ref, win_ref, lng_ref, lnb_ref, ws_ref, bs_ref, wout_ref, gout_ref,
                 o_ref, vo_ref, *, tm):
    W, G, C = GMLP_WIDTH, GMLP_GROUPS, GMLP_CHUNK
    h = h_ref[...]
    y = _rms(h, gin_ref[...]).astype(BF16)
    v = jax.nn.gelu(_dot(y, win_ref[:, W:2 * W]))
    mu = jnp.mean(v, axis=-1, keepdims=True)
    vc = v - mu
    var = jnp.mean(vc * vc, axis=-1, keepdims=True)
    vb = (vc * lax.rsqrt(var + EPS) * lng_ref[...] + lnb_ref[...]).astype(BF16)
    p_out = lax.broadcasted_iota(jnp.int32, (C, C), 0)
    q_in = lax.broadcasted_iota(jnp.int32, (C, C), 1)
    keep = (q_in // CHUNK) <= (p_out // CHUNK)
    for g in range(G):
        wsg = jnp.where(keep, ws_ref[g], 0.0).astype(BF16)
        for n in range(tm // C):
            blk = _dot(wsg, vb[n * C:(n + 1) * C, g * C:(g + 1) * C])
            vo_ref[n * C:(n + 1) * C, g * C:(g + 1) * C] = blk + bs_ref[:, g * C:(g + 1) * C]
    u = jax.nn.gelu(_dot(y, win_ref[:, 0:W]))
    out = _dot((u * vo_ref[...]).astype(BF16), wout_ref[...])
    o_ref[...] = h + _rms(out, gout_ref[...])


def _gmlp_layer(h, g_in, w_in, ln_g, ln_b, w_s, b_s, w_out, g_out):
    B, S, D = h.shape
    tm = min(GMLP_TILE, S)
    assert S % tm == 0 and tm % GMLP_CHUNK == 0
    W = GMLP_WIDTH
    bs = jnp.repeat(b_s.T, GMLP_CHUNK, axis=1)
    return pl.pallas_call(
        functools.partial(_gmlp_kernel, tm=tm),
        out_shape=jax.ShapeDtypeStruct((B, S, D), F32),
        grid=(B, S // tm),
        in_specs=[_rows(tm, D), _resident((1, D)), _resident((D, 2 * W)), _resident((1, W)),
                  _resident((1, W)), _resident(w_s.shape), _resident(bs.shape),
                  _resident((W, D)), _resident((1, D))],
        out_specs=_rows(tm, D),
        scratch_shapes=[pltpu.VMEM((tm, W), F32)],
        compiler_params=_params("parallel", "parallel"),
        name="gmlp_mixer",
    )(h, g_in.reshape(1, D), w_in.astype(BF16), ln_g.reshape(1, W), ln_b.reshape(1, W), w_s, bs,
      w_out.astype(BF16), g_out.reshape(1, D))


def _qkv_kernel(h_ref, gin_ref, wq_ref, wkt_ref, wv_ref, q_ref, kt_ref, v_ref, *, scale):
    y = _rms(h_ref[...], gin_ref[...]).astype(BF16)
    q_ref[...] = (_dot(y, wq_ref[...]) * scale).astype(BF16)
    kt_ref[...] = lax.dot_general(wkt_ref[...], y, (((1,), (1,)), ((), ())),
                                  preferred_element_type=F32).astype(BF16)
    v_ref[...] = _dot(y, wv_ref[...]).astype(BF16)


def _fox_proj_kernel(h_ref, gin_ref, wq_ref, wkt_ref, wv_ref, wf_ref, bf_ref,
                     q_ref, kt_ref, v_ref, cum_ref, cumt_ref, carry_ref, *, scale, tm):
    @pl.when(pl.program_id(1) == 0)
    def _():
        carry_ref[...] = jnp.zeros_like(carry_ref)

    y = _rms(h_ref[...], gin_ref[...]).astype(BF16)
    q_ref[...] = (_dot(y, wq_ref[...]) * scale).astype(BF16)
    kt_ref[...] = lax.dot_general(wkt_ref[...], y, (((1,), (1,)), ((), ())),
                                  preferred_element_type=F32).astype(BF16)
    v_ref[...] = _dot(y, wv_ref[...]).astype(BF16)
    f = _dot(y, wf_ref[...]) + bf_ref[...]
    logf = jnp.minimum(f, 0.0) - jnp.log(1.0 + jnp.exp(-jnp.abs(f)))
    row = lax.broadcasted_iota(jnp.int32, (tm, tm), 0)
    col = lax.broadcasted_iota(jnp.int32, (tm, tm), 1)
    tri = jnp.where(col <= row, 1.0, 0.0).astype(F32)
    cum = jnp.dot(tri, logf, preferred_element_type=F32,
                  precision=lax.Precision.HIGHEST) + carry_ref[0:1, :]
    cum_ref[...] = cum
    cumt_ref[...] = cum.T[0:FOX_HEADS, :]
    carry_ref[...] = jnp.broadcast_to(cum[tm - 1:tm, :], carry_ref.shape)


def _attn_projections(h, g_in, wq, wk, wv, scale, fox=None):
    B, S, D = h.shape
    tm = min(ROW_TILE, S)
    assert S % tm == 0
    nq, nk, nv = wq.shape[1], wk.shape[1], wv.shape[1]
    wkt = wk.T.astype(BF16)
    in_specs = [_rows(tm, D), _resident((1, D)), _resident((D, nq)), _resident((nk, D)),
                _resident((D, nv))]
    out_shape = [jax.ShapeDtypeStruct((B, S, nq), BF16), jax.ShapeDtypeStruct((B, nk, S), BF16),
                 jax.ShapeDtypeStruct((B, S, nv), BF16)]
    out_specs = [_rows(tm, nq), pl.BlockSpec((None, nk, tm), lambda b, s: (b, 0, s)), _rows(tm, nv)]
    args = [h, g_in.reshape(1, D), wq.astype(BF16), wkt, wv.astype(BF16)]
    if fox is None:
        return pl.pallas_call(
            functools.partial(_qkv_kernel, scale=scale),
            out_shape=out_shape, grid=(B, S // tm), in_specs=in_specs, out_specs=out_specs,
            compiler_params=_params("parallel", "parallel"), name="qkv_proj",
        )(*args)
    wf, bf = fox
    H = wf.shape[1]
    wf_pad = jnp.zeros((D, LANES), F32).at[:, :H].set(wf).astype(BF16)
    bf_pad = jnp.zeros((1, LANES), F32).at[0, :H].set(bf)
    in_specs += [_resident((D, LANES)), _resident((1, LANES))]
    out_shape += [jax.ShapeDtypeStruct((B, S, LANES), F32), jax.ShapeDtypeStruct((B, H, S), F32)]
    out_specs += [_rows(tm, LANES), pl.BlockSpec((None, H, tm), lambda b, s: (b, 0, s))]
    return pl.pallas_call(
        functools.partial(_fox_proj_kernel, scale=scale, tm=tm),
        out_shape=out_shape, grid=(B, S // tm), in_specs=in_specs, out_specs=out_specs,
        scratch_shapes=[pltpu.VMEM((SUBLANES, LANES), F32)],
        compiler_params=_params("parallel", "arbitrary"), name="fox_proj",
    )(*args, wf_pad, bf_pad)


def _out_proj_kernel(o_ref, h_ref, w_ref, g_ref, out_ref):
    out_ref[...] = h_ref[...] + _rms(_dot(o_ref[...], w_ref[...]), g_ref[...])


def _out_proj(o, h, w_out, g_out):
    B, S, D = h.shape
    n = o.shape[-1]
    tm = min(ROW_TILE, S)
    return pl.pallas_call(
        _out_proj_kernel,
        out_shape=jax.ShapeDtypeStruct((B, S, D), F32),
        grid=(B, S // tm),
        in_specs=[_rows(tm, n), _rows(tm, D), _resident((n, D)), _resident((1, D))],
        out_specs=_rows(tm, D),
        compiler_params=_params("parallel", "parallel"), name="out_proj",
    )(o, h, w_out.astype(BF16), g_out.reshape(1, D))


def _softmax_step(s, m, l, acc, v_tile):
    m_new = jnp.maximum(m, jnp.max(s, axis=-1, keepdims=True))
    alpha = jnp.exp(m - m_new)
    p = jnp.exp(s - m_new)
    l_new = alpha * l + jnp.sum(p, axis=-1, keepdims=True)
    acc_new = alpha * acc + _dot(p.astype(BF16), v_tile)
    return m_new, l_new, acc_new


def _softmax_init(tq, width):
    return (jnp.full((tq, 1), MASKED, F32), jnp.zeros((tq, 1), F32), jnp.zeros((tq, width), F32))


def _t5_bucket(rel):
    half = REL_BUCKETS // 2
    max_exact = half // 2
    n = jnp.abs(rel)
    ret = jnp.where(rel > 0, half, 0)
    nf = jnp.maximum(n, 1).astype(jnp.float32)
    large = max_exact + (jnp.log(nf / max_exact) / math.log(REL_MAX_DIST / max_exact)
                         * (half - max_exact)).astype(jnp.int32)
    large = jnp.minimum(large, half - 1)
    return ret + jnp.where(n < max_exact, n, large)


def _diff_bias_tiles(rel_bias, t):
    assert t >= REL_MAX_DIST
    qpos = jnp.arange(t)[:, None]
    kpos = jnp.arange(t)[None, :]
    far = rel_bias[_t5_bucket(jnp.int32(-REL_MAX_DIST))]
    diag = rel_bias[_t5_bucket(kpos - qpos)] - far
    diag = jnp.where(((kpos // CHUNK) <= (qpos // CHUNK))[..., None], diag, MASKED)
    prev = rel_bias[_t5_bucket(kpos - qpos - t)] - far
    return jnp.stack([prev, diag], axis=0).transpose(3, 0, 1, 2).astype(F32)


def _diff_attn_kernel(q_ref, kt_ref, v_ref, bias_ref, lam_ref, subg_ref, o_ref, *, t, lambda_init):
    qi = pl.program_id(2)
    q = q_ref[...]
    lane = lax.broadcasted_iota(jnp.int32, q.shape, 1)
    zero = jnp.zeros_like(q)
    qs = (jnp.where(lane < DIFF_HEAD_DIM, q, zero), jnp.where(lane >= DIFF_HEAD_DIM, q, zero))

    def step(j, state, bias):
        k0 = pl.multiple_of(j * t, t)
        kt = kt_ref[:, pl.ds(k0, t)]
        v = v_ref[pl.ds(k0, t), :]
        new = []
        for i in range(2):
            s = _dot(qs[i], kt)
            if bias is not None:
                s = s + bias
            new.append(_softmax_step(s, *state[i], v))
        return tuple(new)

    state = (_softmax_init(t, 2 * DIFF_HEAD_DIM), _softmax_init(t, 2 * DIFF_HEAD_DIM))
    state = lax.fori_loop(0, jnp.maximum(qi - 1, 0), lambda j, st: step(j, st, None), state)
    state = lax.fori_loop(jnp.maximum(qi - 1, 0), qi, lambda j, st: step(j, st, bias_ref[0]), state)
    (_, l0, a0), (_, l1, a1) = step(qi, state, bias_ref[1])

    lam = lam_ref[...]
    lam_full = (jnp.exp(jnp.sum(lam[0:1] * lam[1:2], axis=-1, keepdims=True))
                - jnp.exp(jnp.sum(lam[2:3] * lam[3:4], axis=-1, keepdims=True)) + lambda_init)
    o = a0 / l0 - lam_full * (a1 / l1)
    o_ref[...] = (_rms(o, subg_ref[...]) * (1.0 - lambda_init)).astype(BF16)


def _diff_attention(q, kt, v, rel_bias, lam, sub_g, lambda_init):
    B, S, _ = q.shape
    t = min(ATT_TILE, S)
    assert S % t == 0
    H, W = DIFF_HEADS, 2 * DIFF_HEAD_DIM
    bias = _diff_bias_tiles(rel_bias, t)
    return pl.pallas_call(
        functools.partial(_diff_attn_kernel, t=t, lambda_init=lambda_init),
        out_shape=jax.ShapeDtypeStruct((B, S, H * W), BF16),
        grid=(B, H, S // t),
        in_specs=[pl.BlockSpec((None, t, W), lambda b, h, i: (b, i, h)),
                  pl.BlockSpec((None, W, S), lambda b, h, i: (b, h, 0)),
                  pl.BlockSpec((None, S, W), lambda b, h, i: (b, 0, h)),
                  pl.BlockSpec((None, 2, t, t), lambda b, h, i: (h, 0, 0, 0)),
                  pl.BlockSpec((4, DIFF_HEAD_DIM), lambda b, h, i: (0, 0)),
                  pl.BlockSpec((1, W), lambda b, h, i: (0, 0))],
        out_specs=pl.BlockSpec((None, t, W), lambda b, h, i: (b, i, h)),
        compiler_params=_params("parallel", "parallel", "arbitrary"), name="diff_attention",
    )(q, kt, v, bias, lam, sub_g.reshape(1, W))


def _fox_attn_kernel(q_ref, kt_ref, v_ref, cum_ref, cumt_ref, o_ref, *, t):
    pair = pl.program_id(1)
    qi = pl.program_id(2)
    d = FOX_HEAD_DIM
    q = q_ref[...]
    lane = lax.broadcasted_iota(jnp.int32, q.shape, 1)
    zero = jnp.zeros_like(q)
    qs = (jnp.where(lane < d, q, zero), jnp.where(lane >= d, q, zero))
    cum = cum_ref[...]
    cq = tuple(jnp.sum(jnp.where(lane == 2 * pair + i, cum, 0.0), axis=-1, keepdims=True)
               for i in range(2))

    def step(j, state, causal):
        k0 = pl.multiple_of(j * t, t)
        kt = kt_ref[:, pl.ds(k0, t)]
        v = v_ref[pl.ds(k0, t), :]
        new = []
        for i in range(2):
            s = _dot(qs[i], kt) + (cq[i] - cumt_ref[i:i + 1, pl.ds(k0, t)])
            if causal:
                row = lax.broadcasted_iota(jnp.int32, s.shape, 0)
                col = lax.broadcasted_iota(jnp.int32, s.shape, 1)
                s = jnp.where(col <= row, s, MASKED)
            new.append(_softmax_step(s, *state[i], v))
        return tuple(new)

    state = (_softmax_init(t, 2 * d), _softmax_init(t, 2 * d))
    state = lax.fori_loop(0, qi, lambda j, st: step(j, st, False), state)
    (_, l0, a0), (_, l1, a1) = step(qi, state, True)
    o_ref[...] = jnp.where(lane < d, a0 / l0, a1 / l1).astype(BF16)


def _fox_attention(q, kt, v, cum, cumt):
    B, S, _ = q.shape
    t = min(ATT_TILE, S)
    assert S % t == 0
    H, d = FOX_HEADS, FOX_HEAD_DIM
    W = 2 * d
    cumt = cumt.reshape(B, H // 2, 2, S)
    return pl.pallas_call(
        functools.partial(_fox_attn_kernel, t=t),
        out_shape=jax.ShapeDtypeStruct((B, S, H * d), BF16),
        grid=(B, H // 2, S // t),
        in_specs=[pl.BlockSpec((None, t, W), lambda b, p, i: (b, i, p)),
                  pl.BlockSpec((None, W, S), lambda b, p, i: (b, p, 0)),
                  pl.BlockSpec((None, S, W), lambda b, p, i: (b, 0, p)),
                  pl.BlockSpec((None, t, LANES), lambda b, p, i: (b, i, 0)),
                  pl.BlockSpec((None, None, 2, S), lambda b, p, i: (b, p, 0, 0))],
        out_specs=pl.BlockSpec((None, t, W), lambda b, p, i: (b, i, p)),
        compiler_params=_params("parallel", "parallel", "arbitrary"), name="fox_attention",
    )(q, kt, v, cum, cumt)


def _rglru_kernel(h_ref, gin_ref, win_ref, cw_ref, cb_ref, wr_ref, br_ref, wi_ref, bi_ref, lam_ref,
                  wout_ref, gout_ref, o_ref, ext_ref, xcarry_ref, hcarry_ref, *, tm):
    R = RNN_WIDTH

    @pl.when(pl.program_id(1) == 0)
    def _():
        xcarry_ref[...] = jnp.zeros_like(xcarry_ref)
        hcarry_ref[...] = jnp.zeros_like(hcarry_ref)

    h_in = h_ref[...]
    y = _rms(h_in, gin_ref[...]).astype(BF16)
    x = _dot(y, win_ref[:, R:2 * R])
    ext_ref[0:SUBLANES, :] = xcarry_ref[...]
    ext_ref[SUBLANES:SUBLANES + tm, :] = x
    xcarry_ref[...] = x[tm - SUBLANES:tm, :]
    cw = cw_ref[...]
    xr = x * cw[3:4] + cb_ref[...]
    for k in range(1, RNN_CONV):
        xr = xr + ext_ref[pl.ds(SUBLANES - k, tm), :] * cw[RNN_CONV - 1 - k:RNN_CONV - k]
    xb = xr.astype(BF16)
    r = jax.nn.sigmoid(_dot(xb, wr_ref[...]) + br_ref[...])
    i = jax.nn.sigmoid(_dot(xb, wi_ref[...]) + bi_ref[...])
    lam = lam_ref[...]
    softplus_neg = jnp.maximum(-lam, 0.0) + jnp.log(1.0 + jnp.exp(-jnp.abs(lam)))
    a = jnp.exp((-RGLRU_C) * r * softplus_neg)
    u = jnp.sqrt(1.0 - a * a) * (i * xr)

    row = lax.broadcasted_iota(jnp.int32, (tm, R), 0)
    shift = 1
    while shift < tm:
        live = row >= shift
        a_prev = jnp.where(live, pltpu.roll(a, shift, axis=0), 1.0)
        u_prev = jnp.where(live, pltpu.roll(u, shift, axis=0), 0.0)
        u = a * u_prev + u
        a = a * a_prev
        shift *= 2
    hs = a * hcarry_ref[0:1, :] + u
    hcarry_ref[...] = jnp.broadcast_to(hs[tm - 1:tm, :], hcarry_ref.shape)

    gate = jax.nn.gelu(_dot(y, win_ref[:, 0:R]))
    out = _dot((hs * gate).astype(BF16), wout_ref[...])
    o_ref[...] = h_in + _rms(out, gout_ref[...])


def _block_diag(w):
    n, c, d = w.shape
    eye = jnp.eye(n, dtype=w.dtype)
    return (w[:, :, None, :] * eye[:, None, :, None]).reshape(n * c, n * d)


def _rglru_layer(h, g_in, w_in, conv_w, conv_b, w_r, b_r, w_i, b_i, lam, w_out, g_out):
    B, S, D = h.shape
    tm = min(RNN_TILE, S)
    assert S % tm == 0
    R = RNN_WIDTH
    vec = lambda a: a.reshape(1, -1)
    return pl.pallas_call(
        functools.partial(_rglru_kernel, tm=tm),
        out_shape=jax.ShapeDtypeStruct((B, S, D), F32),
        grid=(B, S // tm),
        in_specs=[_rows(tm, D), _resident((1, D)), _resident((D, 2 * R)), _resident((RNN_CONV, R)),
                  _resident((1, R)), _resident((R, R)), _resident((1, R)), _resident((R, R)),
                  _resident((1, R)), _resident((1, R)), _resident((R, D)), _resident((1, D))],
        out_specs=_rows(tm, D),
        scratch_shapes=[pltpu.VMEM((SUBLANES + tm, R), F32), pltpu.VMEM((SUBLANES, R), F32),
                        pltpu.VMEM((SUBLANES, R), F32)],
        compiler_params=_params("parallel", "arbitrary"), name="rglru_mixer",
    )(h, vec(g_in), w_in.astype(BF16), conv_w, vec(conv_b), _block_diag(w_r).astype(BF16), vec(b_r),
      _block_diag(w_i).astype(BF16), vec(b_i), vec(lam), w_out.astype(BF16), vec(g_out))


def kernel(x, norm_g, ffn_w_up, ffn_conv_w, ffn_conv_b, ffn_w_down, rel_bias, a_w_in, a_ln_g, a_ln_b, a_w_s, a_b_s, a_w_out, b_w_in, b_lam, b_sub_g, b_w_out, c_w_in, c_b_f, c_w_out, d_w_in, d_conv_w, d_conv_b, d_w_r, d_b_r, d_w_i, d_b_i, d_lam, d_w_out):
    depth = norm_g.shape[0]
    h = x
    for layer in range(depth):
        m = layer % N_MIXERS
        j = layer // N_MIXERS
        g = norm_g[layer]
        if m == 0:
            h = _gmlp_layer(h, g[0], a_w_in[j], a_ln_g[j], a_ln_b[j], a_w_s[j], a_b_s[j], a_w_out[j], g[1])
        elif m == 1:
            lambda_init = 0.8 - 0.6 * math.exp(-0.3 * layer)
            n = DIFF_HEADS * 2 * DIFF_HEAD_DIM
            w = b_w_in[j]
            q, kt, v = _attn_projections(h, g[0], w[:, :n], w[:, n:2 * n], w[:, 2 * n:],
                                         DIFF_HEAD_DIM ** -0.5)
            o = _diff_attention(q, kt, v, rel_bias, b_lam[j], b_sub_g[j], lambda_init)
            h = _out_proj(o, h, b_w_out[j], g[1])
        elif m == 2:
            n = FOX_HEADS * FOX_HEAD_DIM
            w = c_w_in[j]
            q, kt, v, cum, cumt = _attn_projections(h, g[0], w[:, :n], w[:, n:2 * n], w[:, 2 * n:3 * n],
                                                    FOX_HEAD_DIM ** -0.5, fox=(w[:, 3 * n:], c_b_f[j]))
            o = _fox_attention(q, kt, v, cum, cumt)
            h = _out_proj(o, h, c_w_out[j], g[1])
        else:
            h = _rglru_layer(h, g[0], d_w_in[j], d_conv_w[j], d_conv_b[j], d_w_r[j], d_b_r[j],
                             d_w_i[j], d_b_i[j], d_lam[j], d_w_out[j], g[1])
        h = _conv_ffn(h, g[2], ffn_w_up[layer], ffn_conv_w[layer], ffn_conv_b[layer],
                      ffn_w_down[layer], g[3])
    return h
```

```python
import functools
import math

import numpy as np
import jax
import jax.numpy as jnp
from jax import lax
from jax.experimental import pallas as pl
from jax.experimental.pallas import tpu as pltpu

F32 = jnp.float32
BF16 = jnp.bfloat16

D_MODEL = 1024
CHUNK = 64
N_MIXERS = 4
EPS = 1e-6
GMLP_CHUNK = 128
GMLP_WIDTH = 1024
GMLP_GROUPS = 8
DIFF_HEADS = 8
DIFF_HEAD_DIM = 64
FOX_HEADS = 16
FOX_HEAD_DIM = 64
RNN_WIDTH = 1280
RNN_BLOCKS = 16
RNN_BLOCK_DIM = RNN_WIDTH // RNN_BLOCKS
RNN_CONV = 4
RGLRU_C = 8.0
D_FF = 2816
FFN_CONV = 3
REL_BUCKETS = 32
REL_MAX_DIST = 128

LANES = 128
SUBLANES = 8
MASKED = -1e30
VMEM_LIMIT = 56 * 1024 * 1024

FFN_TILE = 512
FFN_CHUNK = 256
ROW_TILE = 512
GMLP_TILE = 256
RNN_TILE = 256
ATT_TILE = 512
SOFTMAX_GROUP = 512
DECAY_PIECES = 3
LOG2E = math.log2(math.e)

_NT = (((1,), (1,)), ((), ()))


def _params(*sem):
    return pltpu.CompilerParams(dimension_semantics=sem, vmem_limit_bytes=VMEM_LIMIT)


def _resident(shape):
    zeros = (0,) * len(shape)
    return pl.BlockSpec(shape, lambda *_: zeros, pipeline_mode=pl.Buffered(1))


def _rows(tm, width):
    return pl.BlockSpec((None, tm, width), lambda b, s: (b, s, 0))


def _cols(height, tm):
    return pl.BlockSpec((None, height, tm), lambda b, s: (b, 0, s))


def _rms(x, g):
    return x * lax.rsqrt(jnp.mean(x * x, axis=-1, keepdims=True) + EPS) * g


def _dot(a, b):
    return jnp.dot(a, b, preferred_element_type=F32)


def _ffn_kernel(h_ref, gin_ref, wup_ref, cw_ref, cb_ref, wdn_ref, gout_ref, o_ref,
                carry_ref, ext_ref, acc_ref, y_ref, *, tm, nc):
    @pl.when(pl.program_id(1) == 0)
    def _():
        carry_ref[...] = jnp.zeros_like(carry_ref)

    h = h_ref[...]
    y_ref[...] = _rms(h, gin_ref[...]).astype(BF16)
    acc_ref[...] = jnp.zeros_like(acc_ref)

    def conv(idx, slot):
        up = _dot(y_ref[...], wup_ref[idx])
        ext_ref[slot, 0:SUBLANES, :] = carry_ref[idx]
        ext_ref[slot, SUBLANES:SUBLANES + tm, :] = up
        carry_ref[idx] = up[tm - SUBLANES:tm, :]
        w = cw_ref[idx]
        return (up * w[2:3] + ext_ref[slot, pl.ds(SUBLANES - 1, tm), :] * w[1:2]
                + ext_ref[slot, pl.ds(SUBLANES - 2, tm), :] * w[0:1] + cb_ref[idx])

    def body(c, carry):
        cg = conv(c, 0)
        cu = conv(c + nc, 1)
        act = (jax.nn.gelu(cg) * cu).astype(BF16)
        acc_ref[...] += _dot(act, wdn_ref[c])
        return carry

    lax.fori_loop(0, nc, body, 0)
    o_ref[...] = h + _rms(acc_ref[...], gout_ref[...])


def _conv_ffn(h, g_in, w_up, conv_w, conv_b, w_down, g_out):
    B, S, D = h.shape
    tm, fc = min(FFN_TILE, S), FFN_CHUNK
    nc = D_FF // fc
    assert D_FF % fc == 0 and S % tm == 0
    wup = w_up.astype(BF16).reshape(D, 2 * nc, fc).transpose(1, 0, 2)
    cw = conv_w.reshape(FFN_CONV, 2 * nc, fc).transpose(1, 0, 2)
    cb = conv_b.reshape(2 * nc, 1, fc)
    wdn = w_down.astype(BF16).reshape(nc, fc, D)
    return pl.pallas_call(
        functools.partial(_ffn_kernel, tm=tm, nc=nc),
        out_shape=jax.ShapeDtypeStruct((B, S, D), F32),
        grid=(B, S // tm),
        in_specs=[_rows(tm, D), _resident((1, D)), _resident(wup.shape), _resident(cw.shape),
                  _resident(cb.shape), _resident(wdn.shape), _resident((1, D))],
        out_specs=_rows(tm, D),
        scratch_shapes=[pltpu.VMEM((2 * nc, SUBLANES, fc), F32),
                        pltpu.VMEM((2, SUBLANES + tm, fc), F32),
                        pltpu.VMEM((tm, D), F32),
                        pltpu.VMEM((tm, D), BF16)],
        compiler_params=_params("parallel", "arbitrary"),
        name="conv_ffn",
    )(h, g_in.reshape(1, D), wup, cw, cb, wdn, g_out.reshape(1, D))


def _gmlp_kernel(h_ref, gin_ref, win_ref, lng_ref, lnb_ref, ws_ref, bs_ref, wout_ref, gout_ref,
                 o_ref, vo_ref, *, tm):
    W, G, C = GMLP_WIDTH, GMLP_GROUPS, GMLP_CHUNK
    h = h_ref[...]
    y = _rms(h, gin_ref[...]).astype(BF16)
    v = jax.nn.gelu(_dot(y, win_ref[:, W:2 * W]))
    mu = jnp.mean(v, axis=-1, keepdims=True)
    vc = v - mu
    var = jnp.mean(vc * vc, axis=-1, keepdims=True)
    vb = (vc * lax.rsqrt(var + EPS) * lng_ref[...] + lnb_ref[...]).astype(BF16)
    p_out = lax.broadcasted_iota(jnp.int32, (C, C), 0)
    q_in = lax.broadcasted_iota(jnp.int32, (C, C), 1)
    keep = (q_in // CHUNK) <= (p_out // CHUNK)
    for g in range(G):
        wsg = jnp.where(keep, ws_ref[g], 0.0).astype(BF16)
        for n in range(tm // C):
            blk = _dot(wsg, vb[n * C:(n + 1) * C, g * C:(g + 1) * C])
            vo_ref[n * C:(n + 1) * C, g * C:(g + 1) * C] = blk + bs_ref[:, g * C:(g + 1) * C]
    u = jax.nn.gelu(_dot(y, win_ref[:, 0:W]))
    out = _dot((u * vo_ref[...]).astype(BF16), wout_ref[...])
    o_ref[...] = h + _rms(out, gout_ref[...])


def _gmlp_layer(h, g_in, w_in, ln_g, ln_b, w_s, b_s, w_out, g_out):
    B, S, D = h.shape
    tm = min(GMLP_TILE, S)
    assert S % tm == 0 and tm % GMLP_CHUNK == 0
    W = GMLP_WIDTH
    bs = jnp.repeat(b_s.T, GMLP_CHUNK, axis=1)
    return pl.pallas_call(
        functools.partial(_gmlp_kernel, tm=tm),
        out_shape=jax.ShapeDtypeStruct((B, S, D), F32),
        grid=(B, S // tm),
        in_specs=[_rows(tm, D), _resident((1, D)), _resident((D, 2 * W)), _resident((1, W)),
                  _resident((1, W)), _resident(w_s.shape), _resident(bs.shape),
                  _resident((W, D)), _resident((1, D))],
        out_specs=_rows(tm, D),
        scratch_shapes=[pltpu.VMEM((tm, W), F32)],
        compiler_params=_params("parallel", "parallel"),
        name="gmlp_mixer",
    )(h, g_in.reshape(1, D), w_in.astype(BF16), ln_g.reshape(1, W), ln_b.reshape(1, W), w_s, bs,
      w_out.astype(BF16), g_out.reshape(1, D))


def _project_qkv(h_ref, gin_ref, wqt_ref, wk_ref, wvt_ref, qt_ref, k_ref, vt_ref, scale):
    y = _rms(h_ref[...], gin_ref[...]).astype(BF16)
    qt = lax.dot_general(wqt_ref[...], y, _NT, preferred_element_type=F32)
    qt_ref[...] = (qt * scale).astype(BF16)
    k_ref[...] = _dot(y, wk_ref[...]).astype(BF16)
    vt_ref[...] = lax.dot_general(wvt_ref[...], y, _NT, preferred_element_type=F32).astype(BF16)
    return y


def _qkv_kernel(h_ref, gin_ref, wqt_ref, wk_ref, wvt_ref, qt_ref, k_ref, vt_ref, *, scale):
    _project_qkv(h_ref, gin_ref, wqt_ref, wk_ref, wvt_ref, qt_ref, k_ref, vt_ref, scale)


def _fox_proj_kernel(h_ref, gin_ref, wqt_ref, wk_ref, wvt_ref, wf_ref, bf_ref, place_ref,
                     qt_ref, k_ref, vt_ref, kaug_ref, carry_ref, *, scale, tm):
    @pl.when(pl.program_id(1) == 0)
    def _():
        carry_ref[...] = jnp.zeros_like(carry_ref)

    y = _project_qkv(h_ref, gin_ref, wqt_ref, wk_ref, wvt_ref, qt_ref, k_ref, vt_ref, scale)
    f = _dot(y, wf_ref[...]) + bf_ref[...]
    logf = jnp.minimum(f, 0.0) - jnp.log(1.0 + jnp.exp(-jnp.abs(f)))
    row = lax.broadcasted_iota(jnp.int32, (tm, tm), 0)
    col = lax.broadcasted_iota(jnp.int32, (tm, tm), 1)
    tri = jnp.where(col <= row, 1.0, 0.0).astype(F32)
    cum = jnp.dot(tri, logf, preferred_element_type=F32,
                  precision=lax.Precision.HIGHEST) + carry_ref[0:1, :]
    carry_ref[...] = jnp.broadcast_to(cum[tm - 1:tm, :], carry_ref.shape)
    rest = cum * (-LOG2E)
    aug = jnp.zeros(kaug_ref.shape, F32)
    for c in range(DECAY_PIECES):
        piece = rest.astype(BF16)
        rest = rest - piece.astype(F32)
        aug = aug + _dot(piece, place_ref[c])
    kaug_ref[...] = aug.astype(BF16)


def _decay_placement(n_heads):
    place = np.zeros((DECAY_PIECES, LANES, n_heads // 2 * LANES), np.float32)
    for h in range(n_heads):
        for c in range(DECAY_PIECES):
            place[c, h, (h // 2) * LANES + DECAY_PIECES * (h % 2) + c] = 1.0
    return jnp.asarray(place, BF16)


def _attn_projections(h, g_in, wq, wk, wv, scale, fox=None):
    B, S, D = h.shape
    tm = min(ROW_TILE, S)
    assert S % tm == 0
    nq, nk, nv = wq.shape[1], wk.shape[1], wv.shape[1]
    in_specs = [_rows(tm, D), _resident((1, D)), _resident((nq, D)), _resident((D, nk)),
                _resident((nv, D))]
    out_shape = [jax.ShapeDtypeStruct((B, nq, S), BF16), jax.ShapeDtypeStruct((B, S, nk), BF16),
                 jax.ShapeDtypeStruct((B, nv, S), BF16)]
    out_specs = [_cols(nq, tm), _rows(tm, nk), _cols(nv, tm)]
    args = [h, g_in.reshape(1, D), wq.T.astype(BF16), wk.astype(BF16), wv.T.astype(BF16)]
    if fox is None:
        return pl.pallas_call(
            functools.partial(_qkv_kernel, scale=scale),
            out_shape=out_shape, grid=(B, S // tm), in_specs=in_specs, out_specs=out_specs,
            compiler_params=_params("parallel", "parallel"), name="qkv_proj",
        )(*args)
    wf, bf = fox
    H = wf.shape[1]
    wf_pad = jnp.zeros((D, LANES), F32).at[:, :H].set(wf).astype(BF16)
    bf_pad = jnp.zeros((1, LANES), F32).at[0, :H].set(bf)
    place = _decay_placement(H)
    in_specs += [_resident((D, LANES)), _resident((1, LANES)), _resident(place.shape)]
    out_shape += [jax.ShapeDtypeStruct((B, S, place.shape[2]), BF16)]
    out_specs += [_rows(tm, place.shape[2])]
    return pl.pallas_call(
        functools.partial(_fox_proj_kernel, scale=scale, tm=tm),
        out_shape=out_shape, grid=(B, S // tm), in_specs=in_specs, out_specs=out_specs,
        scratch_shapes=[pltpu.VMEM((SUBLANES, LANES), F32)],
        compiler_params=_params("parallel", "arbitrary"), name="fox_proj",
    )(*args, wf_pad, bf_pad, place)


def _out_proj_kernel(o_ref, h_ref, w_ref, g_ref, out_ref):
    out_ref[...] = h_ref[...] + _rms(_dot(o_ref[...], w_ref[...]), g_ref[...])


def _out_proj(o, h, w_out, g_out):
    B, S, D = h.shape
    n = o.shape[-1]
    tm = min(ROW_TILE, S)
    return pl.pallas_call(
        _out_proj_kernel,
        out_shape=jax.ShapeDtypeStruct((B, S, D), F32),
        grid=(B, S // tm),
        in_specs=[_rows(tm, n), _rows(tm, D), _resident((n, D)), _resident((1, D))],
        out_specs=_rows(tm, D),
        compiler_params=_params("parallel", "parallel"), name="out_proj",
    )(o, h, w_out.astype(BF16), g_out.reshape(1, D))


def _softmax_reset(m_ref, l_ref, acc_ref):
    m_ref[...] = jnp.full(m_ref.shape, MASKED, F32)
    l_ref[...] = jnp.zeros_like(l_ref)
    acc_ref[...] = jnp.zeros_like(acc_ref)


def _attend(keys, vt, qa_ref, m_ref, l_ref, acc_ref, adjust=None):
    width = qa_ref.shape[1]
    group = min(SOFTMAX_GROUP, width)
    for c0 in range(0, width, group):
        cols = slice(c0, c0 + group)
        s = _dot(keys, qa_ref[:, cols])
        if adjust is not None:
            s = adjust(s, c0)
        m_old = m_ref[:, cols]
        m_new = jnp.maximum(m_old, jnp.max(s, axis=0, keepdims=True))
        alpha = jnp.exp2(m_old - m_new)
        p = jnp.exp2(s - m_new)
        l_ref[:, cols] = alpha * l_ref[:, cols] + jnp.sum(p, axis=0, keepdims=True)
        acc_ref[:, cols] = alpha * acc_ref[:, cols] + _dot(vt, p.astype(BF16))
        m_ref[:, cols] = m_new


def _split_rows(qt_ref, qa_ref, d, t):
    qt = qt_ref[...]
    row = lax.broadcasted_iota(jnp.int32, qt.shape, 0)
    zero = jnp.zeros_like(qt)
    qa_ref[0:2 * d, 0:t] = jnp.where(row < d, qt, zero)
    qa_ref[0:2 * d, t:2 * t] = jnp.where(row >= d, qt, zero)


def _t5_bucket(rel):
    half = REL_BUCKETS // 2
    max_exact = half // 2
    n = jnp.abs(rel)
    ret = jnp.where(rel > 0, half, 0)
    nf = jnp.maximum(n, 1).astype(jnp.float32)
    large = max_exact + (jnp.log(nf / max_exact) / math.log(REL_MAX_DIST / max_exact)
                         * (half - max_exact)).astype(jnp.int32)
    large = jnp.minimum(large, half - 1)
    return ret + jnp.where(n < max_exact, n, large)


def _diff_bias_tiles(rel_bias, t):
    assert t >= REL_MAX_DIST
    kpos = jnp.arange(t)[:, None]
    qpos = jnp.arange(t)[None, :]
    rel = jnp.stack([kpos - qpos - t, kpos - qpos], axis=0)
    bucket = _t5_bucket(rel)
    far = rel_bias[_t5_bucket(jnp.int32(-REL_MAX_DIST))]
    bias = jnp.zeros((rel_bias.shape[1],) + bucket.shape, F32)
    for b in range(REL_BUCKETS):
        bias = jnp.where(bucket == b, (rel_bias[b] - far)[:, None, None, None], bias)
    keep = jnp.stack([jnp.ones((t, t), bool), (kpos // CHUNK) <= (qpos // CHUNK)], axis=0)
    return jnp.where(keep, bias * LOG2E, MASKED)


def _diff_attn_kernel(qt_ref, k_ref, vt_ref, bias_ref, lam_ref, subg_ref, o_ref,
                      qa_ref, m_ref, l_ref, acc_ref, *, t, lambda_init):
    qi = pl.program_id(2)
    _split_rows(qt_ref, qa_ref, DIFF_HEAD_DIM, t)
    _softmax_reset(m_ref, l_ref, acc_ref)

    def step(j, tile):
        k0 = pl.multiple_of(j * t, t)
        adjust = None
        if tile is not None:
            def adjust(s, c0):
                if s.shape[1] <= t:
                    return s + bias_ref[tile, :, c0 % t:c0 % t + s.shape[1]]
                b = bias_ref[tile]
                return s + jnp.concatenate([b] * (s.shape[1] // t), axis=1)
        _attend(k_ref[pl.ds(k0, t), :], vt_ref[:, pl.ds(k0, t)], qa_ref, m_ref, l_ref, acc_ref,
                adjust)

    def far(j, carry):
        step(j, None)
        return carry

    def prev(j, carry):
        step(j, 0)
        return carry

    n_far = jnp.maximum(qi - 1, 0)
    lax.fori_loop(0, n_far, far, 0)
    lax.fori_loop(n_far, qi, prev, 0)
    step(qi, 1)

    lam = lam_ref[...]
    lam_full = (jnp.exp(jnp.sum(lam[0:1] * lam[1:2], axis=-1, keepdims=True))
                - jnp.exp(jnp.sum(lam[2:3] * lam[3:4], axis=-1, keepdims=True)) + lambda_init)
    o = acc_ref[...] / l_ref[...]
    o = (o[:, 0:t] - lam_full * o[:, t:2 * t]).T
    o_ref[...] = (_rms(o, subg_ref[...]) * (1.0 - lambda_init)).astype(BF16)


def _diff_attention(qt, k, vt, rel_bias, lam, sub_g, lambda_init):
    B, S, _ = k.shape
    t = min(ATT_TILE, S)
    assert S % t == 0
    H, W = DIFF_HEADS, 2 * DIFF_HEAD_DIM
    bias = _diff_bias_tiles(rel_bias, t)
    return pl.pallas_call(
        functools.partial(_diff_attn_kernel, t=t, lambda_init=lambda_init),
        out_shape=jax.ShapeDtypeStruct((B, S, H * W), BF16),
        grid=(B, H, S // t),
        in_specs=[pl.BlockSpec((None, W, t), lambda b, h, i: (b, h, i)),
                  pl.BlockSpec((None, S, W), lambda b, h, i: (b, 0, h)),
                  pl.BlockSpec((None, W, S), lambda b, h, i: (b, h, 0)),
                  pl.BlockSpec((None, 2, t, t), lambda b, h, i: (h, 0, 0, 0)),
                  pl.BlockSpec((4, DIFF_HEAD_DIM), lambda b, h, i: (0, 0)),
                  pl.BlockSpec((1, W), lambda b, h, i: (0, 0))],
        out_specs=pl.BlockSpec((None, t, W), lambda b, h, i: (b, i, h)),
        scratch_shapes=[pltpu.VMEM((W, 2 * t), BF16), pltpu.VMEM((1, 2 * t), F32),
                        pltpu.VMEM((1, 2 * t), F32), pltpu.VMEM((W, 2 * t), F32)],
        compiler_params=_params("parallel", "parallel", "arbitrary"), name="diff_attention",
    )(qt, k, vt, bias, lam, sub_g.reshape(1, W))


def _fox_attn_kernel(qt_ref, k_ref, kaug_ref, vt_ref, o_ref, qa_ref, m_ref, l_ref, acc_ref, *, t):
    qi = pl.program_id(2)
    d = FOX_HEAD_DIM
    _split_rows(qt_ref, qa_ref, d, t)
    r = lax.broadcasted_iota(jnp.int32, (LANES, 2 * t), 0)
    c = lax.broadcasted_iota(jnp.int32, (LANES, 2 * t), 1)
    first = jnp.where(c < t, 0, DECAY_PIECES)
    qa_ref[2 * d:, :] = jnp.where(r < first, 0.0, jnp.where(r < first + DECAY_PIECES, 1.0, 0.0)
                                  ).astype(BF16)
    _softmax_reset(m_ref, l_ref, acc_ref)

    def step(j, causal):
        k0 = pl.multiple_of(j * t, t)
        keys = jnp.concatenate([k_ref[pl.ds(k0, t), :], kaug_ref[pl.ds(k0, t), :]], axis=1)
        adjust = None
        if causal:
            def adjust(s, c0):
                key = lax.broadcasted_iota(jnp.int32, s.shape, 0)
                qry = (lax.broadcasted_iota(jnp.int32, s.shape, 1) + c0) & (t - 1)
                return jnp.where(key <= qry, s, MASKED)
        _attend(keys, vt_ref[:, pl.ds(k0, t)], qa_ref, m_ref, l_ref, acc_ref, adjust)

    def far(j, carry):
        step(j, False)
        return carry

    lax.fori_loop(0, qi, far, 0)
    step(qi, True)
    o = acc_ref[...] / l_ref[...]
    o = jnp.concatenate([o[0:d, 0:t], o[d:2 * d, t:2 * t]], axis=0)
    o_ref[...] = o.T.astype(BF16)


def _fox_attention(qt, k, kaug, vt):
    B, S, _ = k.shape
    t = min(ATT_TILE, S)
    assert S % t == 0 and (t & (t - 1)) == 0
    H, d = FOX_HEADS, FOX_HEAD_DIM
    W = 2 * d
    return pl.pallas_call(
        functools.partial(_fox_attn_kernel, t=t),
        out_shape=jax.ShapeDtypeStruct((B, S, H * d), BF16),
        grid=(B, H // 2, S // t),
        in_specs=[pl.BlockSpec((None, W, t), lambda b, p, i: (b, p, i)),
                  pl.BlockSpec((None, S, W), lambda b, p, i: (b, 0, p)),
                  pl.BlockSpec((None, S, LANES), lambda b, p, i: (b, 0, p)),
                  pl.BlockSpec((None, W, S), lambda b, p, i: (b, p, 0))],
        out_specs=pl.BlockSpec((None, t, W), lambda b, p, i: (b, i, p)),
        scratch_shapes=[pltpu.VMEM((W + LANES, 2 * t), BF16), pltpu.VMEM((1, 2 * t), F32),
                        pltpu.VMEM((1, 2 * t), F32), pltpu.VMEM((W, 2 * t), F32)],
        compiler_params=_params("parallel", "parallel", "arbitrary"), name="fox_attention",
    )(qt, k, kaug, vt)


def _rglru_kernel(h_ref, gin_ref, win_ref, cw_ref, cb_ref, wr_ref, br_ref, wi_ref, bi_ref, lam_ref,
                  wout_ref, gout_ref, o_ref, ext_ref, xcarry_ref, hcarry_ref, *, tm):
    R = RNN_WIDTH

    @pl.when(pl.program_id(1) == 0)
    def _():
        xcarry_ref[...] = jnp.zeros_like(xcarry_ref)
        hcarry_ref[...] = jnp.zeros_like(hcarry_ref)

    h_in = h_ref[...]
    y = _rms(h_in, gin_ref[...]).astype(BF16)
    x = _dot(y, win_ref[:, R:2 * R])
    ext_ref[0:SUBLANES, :] = xcarry_ref[...]
    ext_ref[SUBLANES:SUBLANES + tm, :] = x
    xcarry_ref[...] = x[tm - SUBLANES:tm, :]
    cw = cw_ref[...]
    xr = x * cw[3:4] + cb_ref[...]
    for k in range(1, RNN_CONV):
        xr = xr + ext_ref[pl.ds(SUBLANES - k, tm), :] * cw[RNN_CONV - 1 - k:RNN_CONV - k]
    xb = xr.astype(BF16)
    r = jax.nn.sigmoid(_dot(xb, wr_ref[...]) + br_ref[...])
    i = jax.nn.sigmoid(_dot(xb, wi_ref[...]) + bi_ref[...])
    lam = lam_ref[...]
    softplus_neg = jnp.maximum(-lam, 0.0) + jnp.log(1.0 + jnp.exp(-jnp.abs(lam)))
    a = jnp.exp((-RGLRU_C) * r * softplus_neg)
    u = jnp.sqrt(1.0 - a * a) * (i * xr)

    row = lax.broadcasted_iota(jnp.int32, (tm, R), 0)
    shift = 1
    while shift < tm:
        live = row >= shift
        a_prev = jnp.where(live, pltpu.roll(a, shift, axis=0), 1.0)
        u_prev = jnp.where(live, pltpu.roll(u, shift, axis=0), 0.0)
        u = a * u_prev + u
        a = a * a_prev
        shift *= 2
    hs = a * hcarry_ref[0:1, :] + u
    hcarry_ref[...] = jnp.broadcast_to(hs[tm - 1:tm, :], hcarry_ref.shape)

    gate = jax.nn.gelu(_dot(y, win_ref[:, 0:R]))
    out = _dot((hs * gate).astype(BF16), wout_ref[...])
    o_ref[...] = h_in + _rms(out, gout_ref[...])


def _block_diag(w):
    n, c, d = w.shape
    eye = jnp.eye(n, dtype=w.dtype)
    return (w[:, :, None, :] * eye[:, None, :, None]).reshape(n * c, n * d)


def _rglru_layer(h, g_in, w_in, conv_w, conv_b, w_r, b_r, w_i, b_i, lam, w_out, g_out):
    B, S, D = h.shape
    tm = min(RNN_TILE, S)
    assert S % tm == 0
    R = RNN_WIDTH
    vec = lambda a: a.reshape(1, -1)
    return pl.pallas_call(
        functools.partial(_rglru_kernel, tm=tm),
        out_shape=jax.ShapeDtypeStruct((B, S, D), F32),
        grid=(B, S // tm),
        in_specs=[_rows(tm, D), _resident((1, D)), _resident((D, 2 * R)), _resident((RNN_CONV, R)),
                  _resident((1, R)), _resident((R, R)), _resident((1, R)), _resident((R, R)),
                  _resident((1, R)), _resident((1, R)), _resident((R, D)), _resident((1, D))],
        out_specs=_rows(tm, D),
        scratch_shapes=[pltpu.VMEM((SUBLANES + tm, R), F32), pltpu.VMEM((SUBLANES, R), F32),
                        pltpu.VMEM((SUBLANES, R), F32)],
        compiler_params=_params("parallel", "arbitrary"), name="rglru_mixer",
    )(h, vec(g_in), w_in.astype(BF16), conv_w, vec(conv_b), _block_diag(w_r).astype(BF16), vec(b_r),
      _block_diag(w_i).astype(BF16), vec(b_i), vec(lam), w_out.astype(BF16), vec(g_out))


def kernel(x, norm_g, ffn_w_up, ffn_conv_w, ffn_conv_b, ffn_w_down, rel_bias, a_w_in, a_ln_g, a_ln_b, a_w_s, a_b_s, a_w_out, b_w_in, b_lam, b_sub_g, b_w_out, c_w_in, c_b_f, c_w_out, d_w_in, d_conv_w, d_conv_b, d_w_r, d_b_r, d_w_i, d_b_i, d_lam, d_w_out):
    depth = norm_g.shape[0]
    h = x
    for layer in range(depth):
        m = layer % N_MIXERS
        j = layer // N_MIXERS
        g = norm_g[layer]
        if m == 0:
            h = _gmlp_layer(h, g[0], a_w_in[j], a_ln_g[j], a_ln_b[j], a_w_s[j], a_b_s[j], a_w_out[j], g[1])
        elif m == 1:
            lambda_init = 0.8 - 0.6 * math.exp(-0.3 * layer)
            n = DIFF_HEADS * 2 * DIFF_HEAD_DIM
            w = b_w_in[j]
            qt, k, vt = _attn_projections(h, g[0], w[:, :n], w[:, n:2 * n], w[:, 2 * n:],
                                          DIFF_HEAD_DIM ** -0.5 * LOG2E)
            o = _diff_attention(qt, k, vt, rel_bias, b_lam[j], b_sub_g[j], lambda_init)
            h = _out_proj(o, h, b_w_out[j], g[1])
        elif m == 2:
            n = FOX_HEADS * FOX_HEAD_DIM
            w = c_w_in[j]
            qt, k, vt, kaug = _attn_projections(h, g[0], w[:, :n], w[:, n:2 * n], w[:, 2 * n:3 * n],
                                                FOX_HEAD_DIM ** -0.5 * LOG2E, fox=(w[:, 3 * n:], c_b_f[j]))
            o = _fox_attention(qt, k, kaug, vt)
            h = _out_proj(o, h, c_w_out[j], g[1])
        else:
            h = _rglru_layer(h, g[0], d_w_in[j], d_conv_w[j], d_conv_b[j], d_w_r[j], d_b_r[j],
                             d_w_i[j], d_b_i[j], d_lam[j], d_w_out[j], g[1])
        h = _conv_ffn(h, g[2], ffn_w_up[layer], ffn_conv_w[layer], ffn_conv_b[layer],
                      ffn_w_down[layer], g[3])
    return h
```

```python
import functools
import math

import numpy as np
import jax
import jax.numpy as jnp
from jax import lax
from jax.experimental import pallas as pl
from jax.experimental.pallas import tpu as pltpu

F32 = jnp.float32
BF16 = jnp.bfloat16

D_MODEL = 1024
CHUNK = 64
N_MIXERS = 4
EPS = 1e-6
GMLP_CHUNK = 128
GMLP_WIDTH = 1024
GMLP_GROUPS = 8
DIFF_HEADS = 8
DIFF_HEAD_DIM = 64
FOX_HEADS = 16
FOX_HEAD_DIM = 64
RNN_WIDTH = 1280
RNN_BLOCKS = 16
RNN_BLOCK_DIM = RNN_WIDTH // RNN_BLOCKS
RNN_CONV = 4
RGLRU_C = 8.0
D_FF = 2816
FFN_CONV = 3
REL_BUCKETS = 32
REL_MAX_DIST = 128

LANES = 128
SUBLANES = 8
MASKED = -1e30
VMEM_LIMIT = 56 * 1024 * 1024

FFN_TILE = 512
FFN_CHUNK = 256
ROW_TILE = 512
GMLP_TILE = 256
RNN_TILE = 256
ATT_TILE = 512
SOFTMAX_GROUP = 256
DECAY_PIECES = 3
LOG2E = math.log2(math.e)

_NT = (((1,), (1,)), ((), ()))


def _params(*sem):
    return pltpu.CompilerParams(dimension_semantics=sem, vmem_limit_bytes=VMEM_LIMIT)


def _resident(shape):
    zeros = (0,) * len(shape)
    return pl.BlockSpec(shape, lambda *_: zeros, pipeline_mode=pl.Buffered(1))


def _rows(tm, width):
    return pl.BlockSpec((None, tm, width), lambda b, s: (b, s, 0))


def _cols(height, tm):
    return pl.BlockSpec((None, height, tm), lambda b, s: (b, 0, s))


def _rms(x, g):
    return x * lax.rsqrt(jnp.mean(x * x, axis=-1, keepdims=True) + EPS) * g


def _dot(a, b):
    return jnp.dot(a, b, preferred_element_type=F32)


def _ffn_kernel(h_ref, gin_ref, wup_ref, cw_ref, cb_ref, wdn_ref, gout_ref, o_ref,
                carry_ref, acc_ref, y_ref, *bufs, tm, nc):
    ext_refs = ((bufs[0], bufs[1]), (bufs[2], bufs[3]))
    act_refs = (bufs[4], bufs[5])

    @pl.when(pl.program_id(1) == 0)
    def _():
        carry_ref[...] = jnp.zeros_like(carry_ref)

    h = h_ref[...]
    y_ref[...] = _rms(h, gin_ref[...]).astype(BF16)

    def up(c, slot):
        for branch in range(2):
            idx = c + branch * nc
            ext = ext_refs[slot][branch]
            u = _dot(y_ref[...], wup_ref[idx])
            ext[0:SUBLANES, :] = carry_ref[idx]
            ext[SUBLANES:SUBLANES + tm, :] = u
            carry_ref[idx] = u[tm - SUBLANES:tm, :]

    def activate(c, slot):
        conv = []
        for branch in range(2):
            idx = c + branch * nc
            ext = ext_refs[slot][branch]
            w = cw_ref[idx]
            conv.append(ext[pl.ds(SUBLANES, tm), :] * w[2:3] + ext[pl.ds(SUBLANES - 1, tm), :] * w[1:2]
                        + ext[pl.ds(SUBLANES - 2, tm), :] * w[0:1] + cb_ref[idx])
        act_refs[slot][...] = (jax.nn.gelu(conv[0]) * conv[1]).astype(BF16)

    def down(c, slot):
        part = _dot(act_refs[slot][...], wdn_ref[c])
        if c == 0:
            acc_ref[...] = part
        else:
            acc_ref[...] += part

    up(0, 0)
    for c in range(nc):
        if c >= 1:
            down(c - 1, (c - 1) % 2)
        if c + 1 < nc:
            up(c + 1, (c + 1) % 2)
        activate(c, c % 2)
    down(nc - 1, (nc - 1) % 2)
    o_ref[...] = h + _rms(acc_ref[...], gout_ref[...])


def _conv_ffn(h, g_in, w_up, conv_w, conv_b, w_down, g_out):
    B, S, D = h.shape
    tm, fc = min(FFN_TILE, S), FFN_CHUNK
    nc = D_FF // fc
    assert D_FF % fc == 0 and S % tm == 0
    wup = w_up.astype(BF16).reshape(D, 2 * nc, fc).transpose(1, 0, 2)
    cw = conv_w.reshape(FFN_CONV, 2 * nc, fc).transpose(1, 0, 2)
    cb = conv_b.reshape(2 * nc, 1, fc)
    wdn = w_down.astype(BF16).reshape(nc, fc, D)
    return pl.pallas_call(
        functools.partial(_ffn_kernel, tm=tm, nc=nc),
        out_shape=jax.ShapeDtypeStruct((B, S, D), F32),
        grid=(B, S // tm),
        in_specs=[_rows(tm, D), _resident((1, D)), _resident(wup.shape), _resident(cw.shape),
                  _resident(cb.shape), _resident(wdn.shape), _resident((1, D))],
        out_specs=_rows(tm, D),
        scratch_shapes=[pltpu.VMEM((2 * nc, SUBLANES, fc), F32),
                        pltpu.VMEM((tm, D), F32),
                        pltpu.VMEM((tm, D), BF16)]
        + [pltpu.VMEM((SUBLANES + tm, fc), F32)] * 4 + [pltpu.VMEM((tm, fc), BF16)] * 2,
        compiler_params=_params("parallel", "arbitrary"),
        name="conv_ffn",
    )(h, g_in.reshape(1, D), wup, cw, cb, wdn, g_out.reshape(1, D))


def _gmlp_kernel(h_ref, gin_ref, win_ref, lng_ref, lnb_ref, ws_ref, bs_ref, wout_ref, gout_ref,
                 o_ref, vo_ref, *, tm):
    W, G, C = GMLP_WIDTH, GMLP_GROUPS, GMLP_CHUNK
    h = h_ref[...]
    y = _rms(h, gin_ref[...]).astype(BF16)
    v = jax.nn.gelu(_dot(y, win_ref[:, W:2 * W]))
    mu = jnp.mean(v, axis=-1, keepdims=True)
    vc = v - mu
    var = jnp.mean(vc * vc, axis=-1, keepdims=True)
    vb = (vc * lax.rsqrt(var + EPS) * lng_ref[...] + lnb_ref[...]).astype(BF16)
    p_out = lax.broadcasted_iota(jnp.int32, (C, C), 0)
    q_in = lax.broadcasted_iota(jnp.int32, (C, C), 1)
    keep = (q_in // CHUNK) <= (p_out // CHUNK)
    for g in range(G):
        wsg = jnp.where(keep, ws_ref[g], 0.0).astype(BF16)
        for n in range(tm // C):
            blk = _dot(wsg, vb[n * C:(n + 1) * C, g * C:(g + 1) * C])
            vo_ref[n * C:(n + 1) * C, g * C:(g + 1) * C] = blk + bs_ref[:, g * C:(g + 1) * C]
    u = jax.nn.gelu(_dot(y, win_ref[:, 0:W]))
    out = _dot((u * vo_ref[...]).astype(BF16), wout_ref[...])
    o_ref[...] = h + _rms(out, gout_ref[...])


def _gmlp_layer(h, g_in, w_in, ln_g, ln_b, w_s, b_s, w_out, g_out):
    B, S, D = h.shape
    tm = min(GMLP_TILE, S)
    assert S % tm == 0 and tm % GMLP_CHUNK == 0
    W = GMLP_WIDTH
    bs = jnp.repeat(b_s.T, GMLP_CHUNK, axis=1)
    return pl.pallas_call(
        functools.partial(_gmlp_kernel, tm=tm),
        out_shape=jax.ShapeDtypeStruct((B, S, D), F32),
        grid=(B, S // tm),
        in_specs=[_rows(tm, D), _resident((1, D)), _resident((D, 2 * W)), _resident((1, W)),
                  _resident((1, W)), _resident(w_s.shape), _resident(bs.shape),
                  _resident((W, D)), _resident((1, D))],
        out_specs=_rows(tm, D),
        scratch_shapes=[pltpu.VMEM((tm, W), F32)],
        compiler_params=_params("parallel", "parallel"),
        name="gmlp_mixer",
    )(h, g_in.reshape(1, D), w_in.astype(BF16), ln_g.reshape(1, W), ln_b.reshape(1, W), w_s, bs,
      w_out.astype(BF16), g_out.reshape(1, D))


def _project_qkv(h_ref, gin_ref, wqt_ref, wk_ref, wvt_ref, qt_ref, k_ref, vt_ref, scale):
    y = _rms(h_ref[...], gin_ref[...]).astype(BF16)
    qt = lax.dot_general(wqt_ref[...], y, _NT, preferred_element_type=F32)
    qt_ref[...] = (qt * scale).astype(BF16)
    k_ref[...] = _dot(y, wk_ref[...]).astype(BF16)
    vt_ref[...] = lax.dot_general(wvt_ref[...], y, _NT, preferred_element_type=F32).astype(BF16)
    return y


def _qkv_kernel(h_ref, gin_ref, wqt_ref, wk_ref, wvt_ref, qt_ref, k_ref, vt_ref, *, scale):
    _project_qkv(h_ref, gin_ref, wqt_ref, wk_ref, wvt_ref, qt_ref, k_ref, vt_ref, scale)


def _fox_proj_kernel(h_ref, gin_ref, wqt_ref, wk_ref, wvt_ref, wf_ref, bf_ref, place_ref,
                     qt_ref, k_ref, vt_ref, kaug_ref, carry_ref, *, scale, tm):
    @pl.when(pl.program_id(1) == 0)
    def _():
        carry_ref[...] = jnp.zeros_like(carry_ref)

    y = _project_qkv(h_ref, gin_ref, wqt_ref, wk_ref, wvt_ref, qt_ref, k_ref, vt_ref, scale)
    f = _dot(y, wf_ref[...]) + bf_ref[...]
    logf = jnp.minimum(f, 0.0) - jnp.log(1.0 + jnp.exp(-jnp.abs(f)))
    row = lax.broadcasted_iota(jnp.int32, (tm, tm), 0)
    col = lax.broadcasted_iota(jnp.int32, (tm, tm), 1)
    tri = jnp.where(col <= row, 1.0, 0.0).astype(F32)
    cum = jnp.dot(tri, logf, preferred_element_type=F32,
                  precision=lax.Precision.HIGHEST) + carry_ref[0:1, :]
    carry_ref[...] = jnp.broadcast_to(cum[tm - 1:tm, :], carry_ref.shape)
    rest = cum * (-LOG2E)
    aug = jnp.zeros(kaug_ref.shape, F32)
    for c in range(DECAY_PIECES):
        piece = rest.astype(BF16)
        rest = rest - piece.astype(F32)
        aug = aug + _dot(piece, place_ref[c])
    kaug_ref[...] = aug.astype(BF16)


def _decay_placement(n_heads):
    place = np.zeros((DECAY_PIECES, LANES, n_heads // 2 * LANES), np.float32)
    for h in range(n_heads):
        for c in range(DECAY_PIECES):
            place[c, h, (h // 2) * LANES + DECAY_PIECES * (h % 2) + c] = 1.0
    return jnp.asarray(place, BF16)


def _attn_projections(h, g_in, wq, wk, wv, scale, fox=None):
    B, S, D = h.shape
    tm = min(ROW_TILE, S)
    assert S % tm == 0
    nq, nk, nv = wq.shape[1], wk.shape[1], wv.shape[1]
    in_specs = [_rows(tm, D), _resident((1, D)), _resident((nq, D)), _resident((D, nk)),
                _resident((nv, D))]
    out_shape = [jax.ShapeDtypeStruct((B, nq, S), BF16), jax.ShapeDtypeStruct((B, S, nk), BF16),
                 jax.ShapeDtypeStruct((B, nv, S), BF16)]
    out_specs = [_cols(nq, tm), _rows(tm, nk), _cols(nv, tm)]
    args = [h, g_in.reshape(1, D), wq.T.astype(BF16), wk.astype(BF16), wv.T.astype(BF16)]
    if fox is None:
        return pl.pallas_call(
            functools.partial(_qkv_kernel, scale=scale),
            out_shape=out_shape, grid=(B, S // tm), in_specs=in_specs, out_specs=out_specs,
            compiler_params=_params("parallel", "parallel"), name="qkv_proj",
        )(*args)
    wf, bf = fox
    H = wf.shape[1]
    wf_pad = jnp.zeros((D, LANES), F32).at[:, :H].set(wf).astype(BF16)
    bf_pad = jnp.zeros((1, LANES), F32).at[0, :H].set(bf)
    place = _decay_placement(H)
    in_specs += [_resident((D, LANES)), _resident((1, LANES)), _resident(place.shape)]
    out_shape += [jax.ShapeDtypeStruct((B, S, place.shape[2]), BF16)]
    out_specs += [_rows(tm, place.shape[2])]
    return pl.pallas_call(
        functools.partial(_fox_proj_kernel, scale=scale, tm=tm),
        out_shape=out_shape, grid=(B, S // tm), in_specs=in_specs, out_specs=out_specs,
        scratch_shapes=[pltpu.VMEM((SUBLANES, LANES), F32)],
        compiler_params=_params("parallel", "arbitrary"), name="fox_proj",
    )(*args, wf_pad, bf_pad, place)


def _out_proj_kernel(o_ref, h_ref, w_ref, g_ref, out_ref):
    out_ref[...] = h_ref[...] + _rms(_dot(o_ref[...], w_ref[...]), g_ref[...])


def _out_proj(o, h, w_out, g_out):
    B, S, D = h.shape
    n = o.shape[-1]
    tm = min(ROW_TILE, S)
    return pl.pallas_call(
        _out_proj_kernel,
        out_shape=jax.ShapeDtypeStruct((B, S, D), F32),
        grid=(B, S // tm),
        in_specs=[_rows(tm, n), _rows(tm, D), _resident((n, D)), _resident((1, D))],
        out_specs=_rows(tm, D),
        compiler_params=_params("parallel", "parallel"), name="out_proj",
    )(o, h, w_out.astype(BF16), g_out.reshape(1, D))


def _softmax_reset(m_ref, l_ref, acc_ref):
    m_ref[...] = jnp.full(m_ref.shape, MASKED, F32)
    l_ref[...] = jnp.zeros_like(l_ref)
    acc_ref[...] = jnp.zeros_like(acc_ref)


def _score_tile(keys, qa_ref, s_refs):
    group = qa_ref.shape[1] // len(s_refs)
    for g, s_ref in enumerate(s_refs):
        s_ref[...] = _dot(keys, qa_ref[:, g * group:(g + 1) * group])


def _consume_tile(vt, next_keys, qa_ref, s_refs, m_refs, l_refs, acc_refs, adjust=None):
    group = qa_ref.shape[1] // len(s_refs)
    for g in range(len(s_refs)):
        s = s_refs[g][...]
        if adjust is not None:
            s = adjust(s, g * group)
        m_old = m_refs[g][...]
        m_new = jnp.maximum(m_old, jnp.max(s, axis=0, keepdims=True))
        alpha = jnp.exp2(m_old - m_new)
        p = jnp.exp2(s - m_new)
        l_refs[g][...] = alpha * l_refs[g][...] + jnp.sum(p, axis=0, keepdims=True)
        m_refs[g][...] = m_new
        if next_keys is not None:
            s_refs[g][...] = _dot(next_keys, qa_ref[:, g * group:(g + 1) * group])
        acc_refs[g][...] = alpha * acc_refs[g][...] + _dot(vt, p.astype(BF16))


def _pipelined_tiles(n_far, keys_of, vt_of, qa_ref, s_refs, m_refs, l_refs, acc_refs, last_adjust,
                     prev_adjust=None):
    state = (qa_ref, s_refs, m_refs, l_refs, acc_refs)
    _score_tile(keys_of(0), qa_ref, s_refs)

    def far(j, carry):
        _consume_tile(vt_of(j), keys_of(j + 1), *state)
        return carry

    if prev_adjust is None:
        lax.fori_loop(0, n_far, far, 0)
    else:
        lax.fori_loop(0, jnp.maximum(n_far - 1, 0), far, 0)

        @pl.when(n_far > 0)
        def _():
            _consume_tile(vt_of(n_far - 1), keys_of(n_far), *state, adjust=prev_adjust)
    _consume_tile(vt_of(n_far), None, *state, adjust=last_adjust)


def _attention_state(t, value_dim):
    groups = 2 * t // SOFTMAX_GROUP
    shapes = ((t, SOFTMAX_GROUP), (1, SOFTMAX_GROUP), (1, SOFTMAX_GROUP), (value_dim, SOFTMAX_GROUP))
    return [pltpu.VMEM(shape, F32) for shape in shapes for _ in range(groups)]


def _split_rows(qt_ref, qa_ref, d, t):
    qt = qt_ref[...]
    row = lax.broadcasted_iota(jnp.int32, qt.shape, 0)
    zero = jnp.zeros_like(qt)
    qa_ref[0:2 * d, 0:t] = jnp.where(row < d, qt, zero)
    qa_ref[0:2 * d, t:2 * t] = jnp.where(row >= d, qt, zero)


def _t5_bucket(rel):
    half = REL_BUCKETS // 2
    max_exact = half // 2
    n = jnp.abs(rel)
    ret = jnp.where(rel > 0, half, 0)
    nf = jnp.maximum(n, 1).astype(jnp.float32)
    large = max_exact + (jnp.log(nf / max_exact) / math.log(REL_MAX_DIST / max_exact)
                         * (half - max_exact)).astype(jnp.int32)
    large = jnp.minimum(large, half - 1)
    return ret + jnp.where(n < max_exact, n, large)


def _diff_bias_tiles(rel_bias, t):
    assert t >= REL_MAX_DIST
    kpos = jnp.arange(t)[:, None]
    qpos = jnp.arange(t)[None, :]
    rel = jnp.stack([kpos - qpos - t, kpos - qpos], axis=0)
    bucket = _t5_bucket(rel)
    far = rel_bias[_t5_bucket(jnp.int32(-REL_MAX_DIST))]
    bias = jnp.zeros((rel_bias.shape[1],) + bucket.shape, F32)
    for b in range(REL_BUCKETS):
        bias = jnp.where(bucket == b, (rel_bias[b] - far)[:, None, None, None], bias)
    keep = jnp.stack([jnp.ones((t, t), bool), (kpos // CHUNK) <= (qpos // CHUNK)], axis=0)
    return jnp.where(keep, bias * LOG2E, MASKED)


def _diff_attn_kernel(qt_ref, k_ref, vt_ref, bias_ref, lam_ref, subg_ref, o_ref, qa_ref, *state,
                      t, lambda_init):
    qi = pl.program_id(2)
    G = 2 * t // SOFTMAX_GROUP
    s_refs, m_refs, l_refs, acc_refs = (state[i * G:(i + 1) * G] for i in range(4))
    _split_rows(qt_ref, qa_ref, DIFF_HEAD_DIM, t)
    for g in range(G):
        _softmax_reset(m_refs[g], l_refs[g], acc_refs[g])

    def add_bias(tile):
        def adjust(s, c0):
            return s + bias_ref[tile, :, c0 % t:c0 % t + s.shape[1]]
        return adjust

    _pipelined_tiles(qi, lambda j: k_ref[pl.ds(pl.multiple_of(j * t, t), t), :],
                     lambda j: vt_ref[:, pl.ds(pl.multiple_of(j * t, t), t)], qa_ref,
                     s_refs, m_refs, l_refs, acc_refs, last_adjust=add_bias(1), prev_adjust=add_bias(0))

    lam = lam_ref[...]
    lam_full = (jnp.exp(jnp.sum(lam[0:1] * lam[1:2], axis=-1, keepdims=True))
                - jnp.exp(jnp.sum(lam[2:3] * lam[3:4], axis=-1, keepdims=True)) + lambda_init)
    o = jnp.concatenate([acc_refs[g][...] / l_refs[g][...] for g in range(G)], axis=1)
    o = (o[:, 0:t] - lam_full * o[:, t:2 * t]).T
    o_ref[...] = (_rms(o, subg_ref[...]) * (1.0 - lambda_init)).astype(BF16)


def _diff_attention(qt, k, vt, rel_bias, lam, sub_g, lambda_init):
    B, S, _ = k.shape
    t = min(ATT_TILE, S)
    assert S % t == 0
    H, W = DIFF_HEADS, 2 * DIFF_HEAD_DIM
    bias = _diff_bias_tiles(rel_bias, t)
    return pl.pallas_call(
        functools.partial(_diff_attn_kernel, t=t, lambda_init=lambda_init),
        out_shape=jax.ShapeDtypeStruct((B, S, H * W), BF16),
        grid=(B, H, S // t),
        in_specs=[pl.BlockSpec((None, W, t), lambda b, h, i: (b, h, i)),
                  pl.BlockSpec((None, S, W), lambda b, h, i: (b, 0, h)),
                  pl.BlockSpec((None, W, S), lambda b, h, i: (b, h, 0)),
                  pl.BlockSpec((None, 2, t, t), lambda b, h, i: (h, 0, 0, 0)),
                  pl.BlockSpec((4, DIFF_HEAD_DIM), lambda b, h, i: (0, 0)),
                  pl.BlockSpec((1, W), lambda b, h, i: (0, 0))],
        out_specs=pl.BlockSpec((None, t, W), lambda b, h, i: (b, i, h)),
        scratch_shapes=[pltpu.VMEM((W, 2 * t), BF16)] + _attention_state(t, W),
        compiler_params=_params("parallel", "parallel", "arbitrary"), name="diff_attention",
    )(qt, k, vt, bias, lam, sub_g.reshape(1, W))


def _fox_attn_kernel(qt_ref, k_ref, kaug_ref, vt_ref, o_ref, qa_ref, *state, t):
    qi = pl.program_id(2)
    d = FOX_HEAD_DIM
    G = 2 * t // SOFTMAX_GROUP
    s_refs, m_refs, l_refs, acc_refs = (state[i * G:(i + 1) * G] for i in range(4))
    _split_rows(qt_ref, qa_ref, d, t)
    r = lax.broadcasted_iota(jnp.int32, (LANES, 2 * t), 0)
    c = lax.broadcasted_iota(jnp.int32, (LANES, 2 * t), 1)
    first = jnp.where(c < t, 0, DECAY_PIECES)
    qa_ref[2 * d:, :] = jnp.where(r < first, 0.0, jnp.where(r < first + DECAY_PIECES, 1.0, 0.0)
                                  ).astype(BF16)
    for g in range(G):
        _softmax_reset(m_refs[g], l_refs[g], acc_refs[g])

    def keys_of(j):
        k0 = pl.multiple_of(j * t, t)
        return jnp.concatenate([k_ref[pl.ds(k0, t), :], kaug_ref[pl.ds(k0, t), :]], axis=1)

    def causal(s, c0):
        key = lax.broadcasted_iota(jnp.int32, s.shape, 0)
        qry = (lax.broadcasted_iota(jnp.int32, s.shape, 1) + c0) & (t - 1)
        return jnp.where(key <= qry, s, MASKED)

    _pipelined_tiles(qi, keys_of, lambda j: vt_ref[:, pl.ds(pl.multiple_of(j * t, t), t)], qa_ref,
                     s_refs, m_refs, l_refs, acc_refs, last_adjust=causal)
    o = jnp.concatenate([acc_refs[g][...] / l_refs[g][...] for g in range(G)], axis=1)
    o = jnp.concatenate([o[0:d, 0:t], o[d:2 * d, t:2 * t]], axis=0)
    o_ref[...] = o.T.astype(BF16)


def _fox_attention(qt, k, kaug, vt):
    B, S, _ = k.shape
    t = min(ATT_TILE, S)
    assert S % t == 0 and (t & (t - 1)) == 0
    H, d = FOX_HEADS, FOX_HEAD_DIM
    W = 2 * d
    return pl.pallas_call(
        functools.partial(_fox_attn_kernel, t=t),
        out_shape=jax.ShapeDtypeStruct((B, S, H * d), BF16),
        grid=(B, H // 2, S // t),
        in_specs=[pl.BlockSpec((None, W, t), lambda b, p, i: (b, p, i)),
                  pl.BlockSpec((None, S, W), lambda b, p, i: (b, 0, p)),
                  pl.BlockSpec((None, S, LANES), lambda b, p, i: (b, 0, p)),
                  pl.BlockSpec((None, W, S), lambda b, p, i: (b, p, 0))],
        out_specs=pl.BlockSpec((None, t, W), lambda b, p, i: (b, i, p)),
        scratch_shapes=[pltpu.VMEM((W + LANES, 2 * t), BF16)] + _attention_state(t, W),
        compiler_params=_params("parallel", "parallel", "arbitrary"), name="fox_attention",
    )(qt, k, kaug, vt)


def _rglru_kernel(h_ref, gin_ref, win_ref, cw_ref, cb_ref, wr_ref, br_ref, wi_ref, bi_ref, lam_ref,
                  wout_ref, gout_ref, o_ref, ext_ref, xcarry_ref, hcarry_ref, *, tm):
    R = RNN_WIDTH

    @pl.when(pl.program_id(1) == 0)
    def _():
        xcarry_ref[...] = jnp.zeros_like(xcarry_ref)
        hcarry_ref[...] = jnp.zeros_like(hcarry_ref)

    h_in = h_ref[...]
    y = _rms(h_in, gin_ref[...]).astype(BF16)
    x = _dot(y, win_ref[:, R:2 * R])
    ext_ref[0:SUBLANES, :] = xcarry_ref[...]
    ext_ref[SUBLANES:SUBLANES + tm, :] = x
    xcarry_ref[...] = x[tm - SUBLANES:tm, :]
    cw = cw_ref[...]
    xr = x * cw[3:4] + cb_ref[...]
    for k in range(1, RNN_CONV):
        xr = xr + ext_ref[pl.ds(SUBLANES - k, tm), :] * cw[RNN_CONV - 1 - k:RNN_CONV - k]
    xb = xr.astype(BF16)
    r = jax.nn.sigmoid(_dot(xb, wr_ref[...]) + br_ref[...])
    i = jax.nn.sigmoid(_dot(xb, wi_ref[...]) + bi_ref[...])
    lam = lam_ref[...]
    softplus_neg = jnp.maximum(-lam, 0.0) + jnp.log(1.0 + jnp.exp(-jnp.abs(lam)))
    a = jnp.exp((-RGLRU_C) * r * softplus_neg)
    u = jnp.sqrt(1.0 - a * a) * (i * xr)

    row = lax.broadcasted_iota(jnp.int32, (tm, R), 0)
    shift = 1
    while shift < tm:
        live = row >= shift
        a_prev = jnp.where(live, pltpu.roll(a, shift, axis=0), 1.0)
        u_prev = jnp.where(live, pltpu.roll(u, shift, axis=0), 0.0)
        u = a * u_prev + u
        a = a * a_prev
        shift *= 2
    hs = a * hcarry_ref[0:1, :] + u
    hcarry_ref[...] = jnp.broadcast_to(hs[tm - 1:tm, :], hcarry_ref.shape)

    gate = jax.nn.gelu(_dot(y, win_ref[:, 0:R]))
    out = _dot((hs * gate).astype(BF16), wout_ref[...])
    o_ref[...] = h_in + _rms(out, gout_ref[...])


def _block_diag(w):
    n, c, d = w.shape
    eye = jnp.eye(n, dtype=w.dtype)
    return (w[:, :, None, :] * eye[:, None, :, None]).reshape(n * c, n * d)


def _rglru_layer(h, g_in, w_in, conv_w, conv_b, w_r, b_r, w_i, b_i, lam, w_out, g_out):
    B, S, D = h.shape
    tm = min(RNN_TILE, S)
    assert S % tm == 0
    R = RNN_WIDTH
    vec = lambda a: a.reshape(1, -1)
    return pl.pallas_call(
        functools.partial(_rglru_kernel, tm=tm),
        out_shape=jax.ShapeDtypeStruct((B, S, D), F32),
        grid=(B, S // tm),
        in_specs=[_rows(tm, D), _resident((1, D)), _resident((D, 2 * R)), _resident((RNN_CONV, R)),
                  _resident((1, R)), _resident((R, R)), _resident((1, R)), _resident((R, R)),
                  _resident((1, R)), _resident((1, R)), _resident((R, D)), _resident((1, D))],
        out_specs=_rows(tm, D),
        scratch_shapes=[pltpu.VMEM((SUBLANES + tm, R), F32), pltpu.VMEM((SUBLANES, R), F32),
                        pltpu.VMEM((SUBLANES, R), F32)],
        compiler_params=_params("parallel", "arbitrary"), name="rglru_mixer",
    )(h, vec(g_in), w_in.astype(BF16), conv_w, vec(conv_b), _block_diag(w_r).astype(BF16), vec(b_r),
      _block_diag(w_i).astype(BF16), vec(b_i), vec(lam), w_out.astype(BF16), vec(g_out))


def kernel(x, norm_g, ffn_w_up, ffn_conv_w, ffn_conv_b, ffn_w_down, rel_bias, a_w_in, a_ln_g, a_ln_b, a_w_s, a_b_s, a_w_out, b_w_in, b_lam, b_sub_g, b_w_out, c_w_in, c_b_f, c_w_out, d_w_in, d_conv_w, d_conv_b, d_w_r, d_b_r, d_w_i, d_b_i, d_lam, d_w_out):
    depth = norm_g.shape[0]
    h = x
    for layer in range(depth):
        m = layer % N_MIXERS
        j = layer // N_MIXERS
        g = norm_g[layer]
        if m == 0:
            h = _gmlp_layer(h, g[0], a_w_in[j], a_ln_g[j], a_ln_b[j], a_w_s[j], a_b_s[j], a_w_out[j], g[1])
        elif m == 1:
            lambda_init = 0.8 - 0.6 * math.exp(-0.3 * layer)
            n = DIFF_HEADS * 2 * DIFF_HEAD_DIM
            w = b_w_in[j]
            qt, k, vt = _attn_projections(h, g[0], w[:, :n], w[:, n:2 * n], w[:, 2 * n:],
                                          DIFF_HEAD_DIM ** -0.5 * LOG2E)
            o = _diff_attention(qt, k, vt, rel_bias, b_lam[j], b_sub_g[j], lambda_init)
            h = _out_proj(o, h, b_w_out[j], g[1])
        elif m == 2:
            n = FOX_HEADS * FOX_HEAD_DIM
            w = c_w_in[j]
            qt, k, vt, kaug = _attn_projections(h, g[0], w[:, :n], w[:, n:2 * n], w[:, 2 * n:3 * n],
                                                FOX_HEAD_DIM ** -0.5 * LOG2E, fox=(w[:, 3 * n:], c_b_f[j]))
            o = _fox_attention(qt, k, kaug, vt)
            h = _out_proj(o, h, c_w_out[j], g[1])
        else:
            h = _rglru_layer(h, g[0], d_w_in[j], d_conv_w[j], d_conv_b[j], d_w_r[j], d_b_r[j],
                             d_w_i[j], d_b_i[j], d_lam[j], d_w_out[j], g[1])
        h = _conv_ffn(h, g[2], ffn_w_up[layer], ffn_conv_w[layer], ffn_conv_b[layer],
                      ffn_w_down[layer], g[3])
    return h
```

```python
import functools
import math

import numpy as np
import jax
import jax.numpy as jnp
from jax import lax
from jax.experimental import pallas as pl
from jax.experimental.pallas import tpu as pltpu

F32 = jnp.float32
BF16 = jnp.bfloat16

D_MODEL = 1024
CHUNK = 64
N_MIXERS = 4
EPS = 1e-6
GMLP_CHUNK = 128
GMLP_WIDTH = 1024
GMLP_GROUPS = 8
DIFF_HEADS = 8
DIFF_HEAD_DIM = 64
FOX_HEADS = 16
FOX_HEAD_DIM = 64
RNN_WIDTH = 1280
RNN_BLOCKS = 16
RNN_BLOCK_DIM = RNN_WIDTH // RNN_BLOCKS
RNN_CONV = 4
RGLRU_C = 8.0
D_FF = 2816
FFN_CONV = 3
REL_BUCKETS = 32
REL_MAX_DIST = 128

LANES = 128
SUBLANES = 8
MXU_TILE = 256
MASKED = -1e30
TINY = 1e-30
VMEM_LIMIT = 56 * 1024 * 1024

FFN_TILE = 512
FFN_CHUNK = 256
FFN_SLOTS = 4
ROW_TILE = 512
GMLP_TILE = 256
RNN_TILE = 256
ATT_TILE = 512
SOFTMAX_GROUP = 256
DECAY_PIECES = 3
CUMSUM_BLOCK = 128
LOG2E = math.log2(math.e)

_NT = (((1,), (1,)), ((), ()))


def _params(*sem):
    return pltpu.CompilerParams(dimension_semantics=sem, vmem_limit_bytes=VMEM_LIMIT)


def _resident(shape):
    zeros = (0,) * len(shape)
    return pl.BlockSpec(shape, lambda *_: zeros, pipeline_mode=pl.Buffered(1))


def _rows(tm, width):
    return pl.BlockSpec((None, tm, width), lambda b, s: (b, s, 0))


def _cols(height, tm):
    return pl.BlockSpec((None, height, tm), lambda b, s: (b, 0, s))


def _rms(x, g):
    return x * lax.rsqrt(jnp.mean(x * x, axis=-1, keepdims=True) + EPS) * g


def _dot(a, b):
    return jnp.dot(a, b, preferred_element_type=F32)


def _ffn_kernel(h_ref, gin_ref, wup_ref, cw_ref, cb_ref, wdn_ref, gout_ref, o_ref,
                carry_ref, acc_ref, y_ref, *bufs, tm, nc):
    ext_refs = tuple((bufs[2 * i], bufs[2 * i + 1]) for i in range(FFN_SLOTS))
    act_refs = bufs[2 * FFN_SLOTS:]

    @pl.when(pl.program_id(1) == 0)
    def _():
        carry_ref[...] = jnp.zeros_like(carry_ref)

    h = h_ref[...]
    y_ref[...] = _rms(h, gin_ref[...]).astype(BF16)

    def up(c, slot):
        for branch in range(2):
            idx = c + branch * nc
            ext = ext_refs[slot][branch]
            u = _dot(y_ref[...], wup_ref[idx])
            ext[0:SUBLANES, :] = carry_ref[idx]
            ext[SUBLANES:SUBLANES + tm, :] = u
            carry_ref[idx] = u[tm - SUBLANES:tm, :]

    def activate(c, slot):
        conv = []
        for branch in range(2):
            idx = c + branch * nc
            ext = ext_refs[slot][branch]
            w = cw_ref[idx]
            conv.append(ext[pl.ds(SUBLANES, tm), :] * w[2:3] + ext[pl.ds(SUBLANES - 1, tm), :] * w[1:2]
                        + ext[pl.ds(SUBLANES - 2, tm), :] * w[0:1] + cb_ref[idx])
        act_refs[slot][...] = (jax.nn.gelu(conv[0]) * conv[1]).astype(BF16)

    def down(c, slot):
        part = _dot(act_refs[slot][...], wdn_ref[c])
        if c == 0:
            acc_ref[...] = part
        else:
            acc_ref[...] += part

    lag = FFN_SLOTS - 1
    for c in range(min(lag, nc)):
        up(c, c % FFN_SLOTS)
    for c in range(nc + lag):
        if lag <= c:
            down(c - lag, (c - lag) % FFN_SLOTS)
        if c + lag < nc:
            up(c + lag, (c + lag) % FFN_SLOTS)
        if c < nc:
            activate(c, c % FFN_SLOTS)
    o_ref[...] = h + _rms(acc_ref[...], gout_ref[...])


def _conv_ffn(h, g_in, w_up, conv_w, conv_b, w_down, g_out):
    B, S, D = h.shape
    tm, fc = min(FFN_TILE, S), FFN_CHUNK
    nc = D_FF // fc
    assert D_FF % fc == 0 and S % tm == 0
    wup = w_up.astype(BF16).reshape(D, 2 * nc, fc).transpose(1, 0, 2)
    cw = conv_w.reshape(FFN_CONV, 2 * nc, fc).transpose(1, 0, 2)
    cb = conv_b.reshape(2 * nc, 1, fc)
    wdn = w_down.astype(BF16).reshape(nc, fc, D)
    return pl.pallas_call(
        functools.partial(_ffn_kernel, tm=tm, nc=nc),
        out_shape=jax.ShapeDtypeStruct((B, S, D), F32),
        grid=(B, S // tm),
        in_specs=[_rows(tm, D), _resident((1, D)), _resident(wup.shape), _resident(cw.shape),
                  _resident(cb.shape), _resident(wdn.shape), _resident((1, D))],
        out_specs=_rows(tm, D),
        scratch_shapes=[pltpu.VMEM((2 * nc, SUBLANES, fc), F32),
                        pltpu.VMEM((tm, D), F32),
                        pltpu.VMEM((tm, D), BF16)]
        + [pltpu.VMEM((SUBLANES + tm, fc), F32)] * (2 * FFN_SLOTS)
        + [pltpu.VMEM((tm, fc), BF16)] * FFN_SLOTS,
        compiler_params=_params("parallel", "arbitrary"),
        name="conv_ffn",
    )(h, g_in.reshape(1, D), wup, cw, cb, wdn, g_out.reshape(1, D))


def _gmlp_kernel(h_ref, gin_ref, win_ref, lng_ref, lnb_ref, ws_ref, bs_ref, wout_ref, gout_ref,
                 o_ref, vo_ref, *, tm):
    W, G, C = GMLP_WIDTH, GMLP_GROUPS, GMLP_CHUNK
    h = h_ref[...]
    y = _rms(h, gin_ref[...]).astype(BF16)
    v = jax.nn.gelu(_dot(y, win_ref[:, W:2 * W]))
    mu = jnp.mean(v, axis=-1, keepdims=True)
    vc = v - mu
    var = jnp.mean(vc * vc, axis=-1, keepdims=True)
    vb = (vc * lax.rsqrt(var + EPS) * lng_ref[...] + lnb_ref[...]).astype(BF16)
    p_out = lax.broadcasted_iota(jnp.int32, (C, C), 0)
    q_in = lax.broadcasted_iota(jnp.int32, (C, C), 1)
    keep = (q_in // CHUNK) <= (p_out // CHUNK)
    for g in range(G):
        wsg = jnp.where(keep, ws_ref[g], 0.0).astype(BF16)
        for n in range(tm // C):
            blk = _dot(wsg, vb[n * C:(n + 1) * C, g * C:(g + 1) * C])
            vo_ref[n * C:(n + 1) * C, g * C:(g + 1) * C] = blk + bs_ref[:, g * C:(g + 1) * C]
    u = jax.nn.gelu(_dot(y, win_ref[:, 0:W]))
    out = _dot((u * vo_ref[...]).astype(BF16), wout_ref[...])
    o_ref[...] = h + _rms(out, gout_ref[...])


def _gmlp_layer(h, g_in, w_in, ln_g, ln_b, w_s, b_s, w_out, g_out):
    B, S, D = h.shape
    tm = min(GMLP_TILE, S)
    assert S % tm == 0 and tm % GMLP_CHUNK == 0
    W = GMLP_WIDTH
    bs = jnp.repeat(b_s.T, GMLP_CHUNK, axis=1)
    return pl.pallas_call(
        functools.partial(_gmlp_kernel, tm=tm),
        out_shape=jax.ShapeDtypeStruct((B, S, D), F32),
        grid=(B, S // tm),
        in_specs=[_rows(tm, D), _resident((1, D)), _resident((D, 2 * W)), _resident((1, W)),
                  _resident((1, W)), _resident(w_s.shape), _resident(bs.shape),
                  _resident((W, D)), _resident((1, D))],
        out_specs=_rows(tm, D),
        scratch_shapes=[pltpu.VMEM((tm, W), F32)],
        compiler_params=_params("parallel", "parallel"),
        name="gmlp_mixer",
    )(h, g_in.reshape(1, D), w_in.astype(BF16), ln_g.reshape(1, W), ln_b.reshape(1, W), w_s, bs,
      w_out.astype(BF16), g_out.reshape(1, D))


def _project_qkv(h_ref, gin_ref, wqt_ref, wk_ref, wvt_ref, qt_ref, k_ref, vt_ref, scale):
    y = _rms(h_ref[...], gin_ref[...]).astype(BF16)
    qt = lax.dot_general(wqt_ref[...], y, _NT, preferred_element_type=F32)
    qt_ref[...] = (qt * scale).astype(BF16)
    k_ref[...] = _dot(y, wk_ref[...]).astype(BF16)
    vt_ref[...] = lax.dot_general(wvt_ref[...], y, _NT, preferred_element_type=F32).astype(BF16)
    return y


def _qkv_kernel(h_ref, gin_ref, wqt_ref, wk_ref, wvt_ref, qt_ref, k_ref, vt_ref, *, scale):
    _project_qkv(h_ref, gin_ref, wqt_ref, wk_ref, wvt_ref, qt_ref, k_ref, vt_ref, scale)


def _fox_proj_kernel(h_ref, gin_ref, wqt_ref, wk_ref, wvt_ref, wf_ref, bf_ref, place_ref,
                     qt_ref, k_ref, vt_ref, kaug_ref, carry_ref, *, scale, tm):
    @pl.when(pl.program_id(1) == 0)
    def _():
        carry_ref[...] = jnp.zeros_like(carry_ref)

    y = _project_qkv(h_ref, gin_ref, wqt_ref, wk_ref, wvt_ref, qt_ref, k_ref, vt_ref, scale)
    f = _dot(y, wf_ref[...]) + bf_ref[...]
    logf = jnp.minimum(f, 0.0) - jnp.log(1.0 + jnp.exp(-jnp.abs(f)))
    n = CUMSUM_BLOCK
    row = lax.broadcasted_iota(jnp.int32, (n, n), 0)
    col = lax.broadcasted_iota(jnp.int32, (n, n), 1)
    tri = jnp.where(col <= row, 1.0, 0.0).astype(BF16)
    total = carry_ref[0:1, :]
    for r0 in range(0, tm, n):
        block = _dot_pieces(tri, logf[r0:r0 + n, :]) + total
        kaug_ref[r0:r0 + n, :] = _dot_pieces(block * (-LOG2E), place_ref, left=True).astype(BF16)
        total = block[n - 1:n, :]
    carry_ref[...] = jnp.broadcast_to(total, carry_ref.shape)


def _dot_pieces(a, b, left=False):
    rest = a if left else b
    out = None
    for c in range(DECAY_PIECES):
        piece = rest.astype(BF16)
        rest = rest - piece.astype(F32)
        term = _dot(piece, b[c]) if left else _dot(a, piece)
        out = term if out is None else out + term
    return out


def _decay_placement(n_heads):
    assert DECAY_PIECES * n_heads <= LANES
    place = np.zeros((DECAY_PIECES, LANES, LANES), np.float32)
    for h in range(n_heads):
        for c in range(DECAY_PIECES):
            place[c, h, DECAY_PIECES * h + c] = 1.0
    return jnp.asarray(place, BF16)


def _attn_projections(h, g_in, wq, wk, wv, scale, fox=None):
    B, S, D = h.shape
    tm = min(ROW_TILE, S)
    assert S % tm == 0
    nq, nk, nv = wq.shape[1], wk.shape[1], wv.shape[1]
    in_specs = [_rows(tm, D), _resident((1, D)), _resident((nq, D)), _resident((D, nk)),
                _resident((nv, D))]
    out_shape = [jax.ShapeDtypeStruct((B, nq, S), BF16), jax.ShapeDtypeStruct((B, S, nk), BF16),
                 jax.ShapeDtypeStruct((B, nv, S), BF16)]
    out_specs = [_cols(nq, tm), _rows(tm, nk), _cols(nv, tm)]
    args = [h, g_in.reshape(1, D), wq.T.astype(BF16), wk.astype(BF16), wv.T.astype(BF16)]
    if fox is None:
        return pl.pallas_call(
            functools.partial(_qkv_kernel, scale=scale),
            out_shape=out_shape, grid=(B, S // tm), in_specs=in_specs, out_specs=out_specs,
            compiler_params=_params("parallel", "parallel"), name="qkv_proj",
        )(*args)
    wf, bf = fox
    H = wf.shape[1]
    wf_pad = jnp.zeros((D, LANES), F32).at[:, :H].set(wf).astype(BF16)
    bf_pad = jnp.zeros((1, LANES), F32).at[0, :H].set(bf)
    place = _decay_placement(H)
    in_specs += [_resident((D, LANES)), _resident((1, LANES)), _resident(place.shape)]
    out_shape += [jax.ShapeDtypeStruct((B, S, place.shape[2]), BF16)]
    out_specs += [_rows(tm, place.shape[2])]
    return pl.pallas_call(
        functools.partial(_fox_proj_kernel, scale=scale, tm=tm),
        out_shape=out_shape, grid=(B, S // tm), in_specs=in_specs, out_specs=out_specs,
        scratch_shapes=[pltpu.VMEM((SUBLANES, LANES), F32)],
        compiler_params=_params("parallel", "arbitrary"), name="fox_proj",
    )(*args, wf_pad, bf_pad, place)


def _out_proj_kernel(o_ref, h_ref, w_ref, g_ref, out_ref):
    out_ref[...] = h_ref[...] + _rms(_dot(o_ref[...], w_ref[...]), g_ref[...])


def _out_proj(o, h, w_out, g_out):
    B, S, D = h.shape
    n = o.shape[-1]
    tm = min(ROW_TILE, S)
    return pl.pallas_call(
        _out_proj_kernel,
        out_shape=jax.ShapeDtypeStruct((B, S, D), F32),
        grid=(B, S // tm),
        in_specs=[_rows(tm, n), _rows(tm, D), _resident((n, D)), _resident((1, D))],
        out_specs=_rows(tm, D),
        compiler_params=_params("parallel", "parallel"), name="out_proj",
    )(o, h, w_out.astype(BF16), g_out.reshape(1, D))


def _softmax_reset(m_ref, l_ref, acc_ref):
    m_ref[...] = jnp.full(m_ref.shape, MASKED, F32)
    l_ref[...] = jnp.zeros_like(l_ref)
    acc_ref[...] = jnp.zeros_like(acc_ref)


def _score_tile(keys, qa_ref, s_refs):
    group = qa_ref.shape[1] // len(s_refs)
    for g, s_ref in enumerate(s_refs):
        s_ref[...] = _dot(keys, qa_ref[:, g * group:(g + 1) * group])


def _consume_tile(vt, next_keys, qa_ref, s_refs, m_refs, l_refs, acc_refs, adjust=None):
    group = qa_ref.shape[1] // len(s_refs)
    pending = None

    def accumulate(g, alpha, p):
        acc_refs[g][...] = alpha * acc_refs[g][...] + _dot(vt, p)

    for g in range(len(s_refs)):
        s = s_refs[g][...]
        if adjust is not None:
            s = adjust(s, g * group)
        m_old = m_refs[g][...]
        m_new = jnp.maximum(m_old, jnp.max(s, axis=0, keepdims=True))
        alpha = jnp.exp2(m_old - m_new)
        p = jnp.exp2(s - m_new)
        l_refs[g][...] = alpha * l_refs[g][...] + jnp.sum(p, axis=0, keepdims=True)
        m_refs[g][...] = m_new
        if next_keys is not None:
            s_refs[g][...] = _dot(next_keys, qa_ref[:, g * group:(g + 1) * group])
        if pending is not None:
            accumulate(*pending)
        pending = (g, alpha, p.astype(BF16))
    accumulate(*pending)


def _pipelined_tiles(n_far, keys_of, vt_of, qa_ref, s_refs, m_refs, l_refs, acc_refs, last_adjust,
                     prev_adjust=None):
    state = (qa_ref, s_refs, m_refs, l_refs, acc_refs)
    _score_tile(keys_of(0), qa_ref, s_refs)

    def far(j, carry):
        _consume_tile(vt_of(j), keys_of(j + 1), *state)
        return carry

    if prev_adjust is None:
        lax.fori_loop(0, n_far, far, 0)
    else:
        lax.fori_loop(0, jnp.maximum(n_far - 1, 0), far, 0)

        @pl.when(n_far > 0)
        def _():
            _consume_tile(vt_of(n_far - 1), keys_of(n_far), *state, adjust=prev_adjust)
    _consume_tile(vt_of(n_far), None, *state, adjust=last_adjust)


def _attention_state(t, value_dim):
    groups = 2 * t // SOFTMAX_GROUP
    shapes = ((t, SOFTMAX_GROUP), (1, SOFTMAX_GROUP), (1, SOFTMAX_GROUP), (value_dim, SOFTMAX_GROUP))
    return [pltpu.VMEM(shape, F32) for shape in shapes for _ in range(groups)]


def _split_rows(qt_ref, qa_ref, d, t):
    qt = qt_ref[...]
    row = lax.broadcasted_iota(jnp.int32, qt.shape, 0)
    zero = jnp.zeros_like(qt)
    qa_ref[0:2 * d, 0:t] = jnp.where(row < d, qt, zero)
    qa_ref[0:2 * d, t:2 * t] = jnp.where(row >= d, qt, zero)


def _t5_bucket(rel):
    half = REL_BUCKETS // 2
    max_exact = half // 2
    n = jnp.abs(rel)
    ret = jnp.where(rel > 0, half, 0)
    nf = jnp.maximum(n, 1).astype(jnp.float32)
    large = max_exact + (jnp.log(nf / max_exact) / math.log(REL_MAX_DIST / max_exact)
                         * (half - max_exact)).astype(jnp.int32)
    large = jnp.minimum(large, half - 1)
    return ret + jnp.where(n < max_exact, n, large)


def _diff_bias_tiles(rel_bias, t):
    assert t >= REL_MAX_DIST
    kpos = jnp.arange(t)[:, None]
    qpos = jnp.arange(t)[None, :]
    rel = jnp.stack([kpos - qpos - t, kpos - qpos], axis=0)
    bucket = _t5_bucket(rel)
    far = rel_bias[_t5_bucket(jnp.int32(-REL_MAX_DIST))]
    bias = jnp.zeros((rel_bias.shape[1],) + bucket.shape, F32)
    for b in range(REL_BUCKETS):
        bias = jnp.where(bucket == b, (rel_bias[b] - far)[:, None, None, None], bias)
    keep = jnp.stack([jnp.ones((t, t), bool), (kpos // CHUNK) <= (qpos // CHUNK)], axis=0)
    return jnp.where(keep, bias * LOG2E, MASKED)


def _diff_attn_kernel(qt_ref, k_ref, vt_ref, bias_ref, lam_ref, subg_ref, o_ref, qa_ref, *state,
                      t, lambda_init):
    qi = pl.program_id(2)
    G = 2 * t // SOFTMAX_GROUP
    s_refs, m_refs, l_refs, acc_refs = (state[i * G:(i + 1) * G] for i in range(4))
    _split_rows(qt_ref, qa_ref, DIFF_HEAD_DIM, t)
    for g in range(G):
        _softmax_reset(m_refs[g], l_refs[g], acc_refs[g])

    def add_bias(tile):
        def adjust(s, c0):
            return s + bias_ref[tile, :, c0 % t:c0 % t + s.shape[1]]
        return adjust

    _pipelined_tiles(qi, lambda j: k_ref[pl.ds(pl.multiple_of(j * t, t), t), :],
                     lambda j: vt_ref[:, pl.ds(pl.multiple_of(j * t, t), t)], qa_ref,
                     s_refs, m_refs, l_refs, acc_refs, last_adjust=add_bias(1), prev_adjust=add_bias(0))

    lam = lam_ref[...]
    lam_full = (jnp.exp(jnp.sum(lam[0:1] * lam[1:2], axis=-1, keepdims=True))
                - jnp.exp(jnp.sum(lam[2:3] * lam[3:4], axis=-1, keepdims=True)) + lambda_init)
    o = jnp.concatenate([acc_refs[g][...] / l_refs[g][...] for g in range(G)], axis=1)
    o = (o[:, 0:t] - lam_full * o[:, t:2 * t]).T
    o_ref[...] = (_rms(o, subg_ref[...]) * (1.0 - lambda_init)).astype(BF16)


def _diff_attention(qt, k, vt, rel_bias, lam, sub_g, lambda_init):
    B, S, _ = k.shape
    t = min(ATT_TILE, S)
    assert S % t == 0
    H, W = DIFF_HEADS, 2 * DIFF_HEAD_DIM
    bias = _diff_bias_tiles(rel_bias, t)
    return pl.pallas_call(
        functools.partial(_diff_attn_kernel, t=t, lambda_init=lambda_init),
        out_shape=jax.ShapeDtypeStruct((B, S, H * W), BF16),
        grid=(B, H, S // t),
        in_specs=[pl.BlockSpec((None, W, t), lambda b, h, i: (b, h, i)),
                  pl.BlockSpec((None, S, W), lambda b, h, i: (b, 0, h)),
                  pl.BlockSpec((None, W, S), lambda b, h, i: (b, h, 0)),
                  pl.BlockSpec((None, 2, t, t), lambda b, h, i: (h, 0, 0, 0)),
                  pl.BlockSpec((4, DIFF_HEAD_DIM), lambda b, h, i: (0, 0)),
                  pl.BlockSpec((1, W), lambda b, h, i: (0, 0))],
        out_specs=pl.BlockSpec((None, t, W), lambda b, h, i: (b, i, h)),
        scratch_shapes=[pltpu.VMEM((W, 2 * t), BF16)] + _attention_state(t, W),
        compiler_params=_params("parallel", "parallel", "arbitrary"), name="diff_attention",
    )(qt, k, vt, bias, lam, sub_g.reshape(1, W))


def _fox_attn_kernel(qt_ref, k_ref, kaug_ref, vt_ref, o_ref, qa_ref, *state, t):
    qi = pl.program_id(2)
    d = FOX_HEAD_DIM
    G = 2 * t // SOFTMAX_GROUP
    s_refs, m_refs, l_refs, acc_refs = (state[i * G:(i + 1) * G] for i in range(4))
    _split_rows(qt_ref, qa_ref, d, t)
    r = lax.broadcasted_iota(jnp.int32, (LANES, 2 * t), 0)
    c = lax.broadcasted_iota(jnp.int32, (LANES, 2 * t), 1)
    first = DECAY_PIECES * (2 * pl.program_id(1) + jnp.where(c < t, 0, 1))
    qa_ref[2 * d:, :] = jnp.where(r < first, 0.0, jnp.where(r < first + DECAY_PIECES, 1.0, 0.0)
                                  ).astype(BF16)
    for g in range(G):
        _softmax_reset(m_refs[g], l_refs[g], acc_refs[g])

    def keys_of(j):
        k0 = pl.multiple_of(j * t, t)
        return jnp.concatenate([k_ref[pl.ds(k0, t), :], kaug_ref[pl.ds(k0, t), :]], axis=1)

    def causal(s, c0):
        key = lax.broadcasted_iota(jnp.int32, s.shape, 0)
        qry = (lax.broadcasted_iota(jnp.int32, s.shape, 1) + c0) & (t - 1)
        return jnp.where(key <= qry, s, MASKED)

    _pipelined_tiles(qi, keys_of, lambda j: vt_ref[:, pl.ds(pl.multiple_of(j * t, t), t)], qa_ref,
                     s_refs, m_refs, l_refs, acc_refs, last_adjust=causal)
    o = jnp.concatenate([acc_refs[g][...] / l_refs[g][...] for g in range(G)], axis=1)
    o = jnp.concatenate([o[0:d, 0:t], o[d:2 * d, t:2 * t]], axis=0)
    o_ref[...] = o.T.astype(BF16)


def _fox_attention(qt, k, kaug, vt):
    B, S, _ = k.shape
    t = min(ATT_TILE, S)
    assert S % t == 0 and (t & (t - 1)) == 0
    H, d = FOX_HEADS, FOX_HEAD_DIM
    W = 2 * d
    return pl.pallas_call(
        functools.partial(_fox_attn_kernel, t=t),
        out_shape=jax.ShapeDtypeStruct((B, S, H * d), BF16),
        grid=(B, H // 2, S // t),
        in_specs=[pl.BlockSpec((None, W, t), lambda b, p, i: (b, p, i)),
                  pl.BlockSpec((None, S, W), lambda b, p, i: (b, 0, p)),
                  pl.BlockSpec((None, S, LANES), lambda b, p, i: (b, 0, 0)),
                  pl.BlockSpec((None, W, S), lambda b, p, i: (b, p, 0))],
        out_specs=pl.BlockSpec((None, t, W), lambda b, p, i: (b, i, p)),
        scratch_shapes=[pltpu.VMEM((W + LANES, 2 * t), BF16)] + _attention_state(t, W),
        compiler_params=_params("parallel", "parallel", "arbitrary"), name="fox_attention",
    )(qt, k, kaug, vt)


def _block_diag_dot(x, w_ref):
    n, bd, tile = w_ref.shape[1], RNN_BLOCK_DIM, MXU_TILE
    outs = []
    for c0 in range(0, n, tile):
        c1 = min(c0 + tile, n)
        k0 = (c0 // bd) * bd // tile * tile
        k1 = min(-(-(-(-c1 // bd) * bd) // tile) * tile, n)
        outs.append(_dot(x[:, k0:k1], w_ref[k0:k1, c0:c1]))
    return jnp.concatenate(outs, axis=1)


def _rglru_kernel(h_ref, gin_ref, win_ref, cw_ref, cb_ref, wr_ref, br_ref, wi_ref, bi_ref, lam_ref,
                  wout_ref, gout_ref, o_ref, ext_ref, xcarry_ref, hcarry_ref, hs_ref, *, tm):
    R = RNN_WIDTH

    @pl.when(pl.program_id(1) == 0)
    def _():
        xcarry_ref[...] = jnp.zeros_like(xcarry_ref)
        hcarry_ref[...] = jnp.zeros_like(hcarry_ref)

    h_in = h_ref[...]
    y = _rms(h_in, gin_ref[...]).astype(BF16)
    x = _dot(y, win_ref[:, R:2 * R])
    ext_ref[0:SUBLANES, :] = xcarry_ref[...]
    ext_ref[SUBLANES:SUBLANES + tm, :] = x
    xcarry_ref[...] = x[tm - SUBLANES:tm, :]
    cw = cw_ref[...]
    xr = x * cw[3:4] + cb_ref[...]
    for k in range(1, RNN_CONV):
        xr = xr + ext_ref[pl.ds(SUBLANES - k, tm), :] * cw[RNN_CONV - 1 - k:RNN_CONV - k]
    xb = xr.astype(BF16)
    r = jax.nn.sigmoid(_block_diag_dot(xb, wr_ref) + br_ref[...])
    i = jax.nn.sigmoid(_block_diag_dot(xb, wi_ref) + bi_ref[...])
    lam = lam_ref[...]
    softplus_neg = jnp.maximum(-lam, 0.0) + jnp.log(1.0 + jnp.exp(-jnp.abs(lam)))
    a = jnp.exp((-RGLRU_C) * r * softplus_neg)
    one_minus_a2 = 1.0 - a * a
    u = one_minus_a2 * lax.rsqrt(jnp.maximum(one_minus_a2, TINY)) * (i * xr)

    row = lax.broadcasted_iota(jnp.int32, (tm, R), 0) % SUBLANES
    shift = 1
    while shift < SUBLANES:
        live = row >= shift
        a_prev = jnp.where(live, pltpu.roll(a, shift, axis=0), 1.0)
        u_prev = jnp.where(live, pltpu.roll(u, shift, axis=0), 0.0)
        u = a * u_prev + u
        a = a * a_prev
        shift *= 2
    carry = hcarry_ref[...]
    for g in range(tm // SUBLANES):
        rows = slice(g * SUBLANES, (g + 1) * SUBLANES)
        hg = a[rows] * carry + u[rows]
        hs_ref[rows, :] = hg
        carry = jnp.broadcast_to(hg[SUBLANES - 1:SUBLANES, :], (SUBLANES, R))
    hcarry_ref[...] = carry
    hs = hs_ref[...]

    gate = jax.nn.gelu(_dot(y, win_ref[:, 0:R]))
    out = _dot((hs * gate).astype(BF16), wout_ref[...])
    o_ref[...] = h_in + _rms(out, gout_ref[...])


def _block_diag(w):
    n, c, d = w.shape
    eye = jnp.eye(n, dtype=w.dtype)
    return (w[:, :, None, :] * eye[:, None, :, None]).reshape(n * c, n * d)


def _rglru_layer(h, g_in, w_in, conv_w, conv_b, w_r, b_r, w_i, b_i, lam, w_out, g_out):
    B, S, D = h.shape
    tm = min(RNN_TILE, S)
    assert S % tm == 0
    R = RNN_WIDTH
    vec = lambda a: a.reshape(1, -1)
    return pl.pallas_call(
        functools.partial(_rglru_kernel, tm=tm),
        out_shape=jax.ShapeDtypeStruct((B, S, D), F32),
        grid=(B, S // tm),
        in_specs=[_rows(tm, D), _resident((1, D)), _resident((D, 2 * R)), _resident((RNN_CONV, R)),
                  _resident((1, R)), _resident((R, R)), _resident((1, R)), _resident((R, R)),
                  _resident((1, R)), _resident((1, R)), _resident((R, D)), _resident((1, D))],
        out_specs=_rows(tm, D),
        scratch_shapes=[pltpu.VMEM((SUBLANES + tm, R), F32), pltpu.VMEM((SUBLANES, R), F32),
                        pltpu.VMEM((SUBLANES, R), F32), pltpu.VMEM((tm, R), F32)],
        compiler_params=_params("parallel", "arbitrary"), name="rglru_mixer",
    )(h, vec(g_in), w_in.astype(BF16), conv_w, vec(conv_b), _block_diag(w_r).astype(BF16), vec(b_r),
      _block_diag(w_i).astype(BF16), vec(b_i), vec(lam), w_out.astype(BF16), vec(g_out))


def kernel(x, norm_g, ffn_w_up, ffn_conv_w, ffn_conv_b, ffn_w_down, rel_bias, a_w_in, a_ln_g, a_ln_b, a_w_s, a_b_s, a_w_out, b_w_in, b_lam, b_sub_g, b_w_out, c_w_in, c_b_f, c_w_out, d_w_in, d_conv_w, d_conv_b, d_w_r, d_b_r, d_w_i, d_b_i, d_lam, d_w_out):
    depth = norm_g.shape[0]
    h = x
    for layer in range(depth):
        m = layer % N_MIXERS
        j = layer // N_MIXERS
        g = norm_g[layer]
        if m == 0:
            h = _gmlp_layer(h, g[0], a_w_in[j], a_ln_g[j], a_ln_b[j], a_w_s[j], a_b_s[j], a_w_out[j], g[1])
        elif m == 1:
            lambda_init = 0.8 - 0.6 * math.exp(-0.3 * layer)
            n = DIFF_HEADS * 2 * DIFF_HEAD_DIM
            w = b_w_in[j]
            qt, k, vt = _attn_projections(h, g[0], w[:, :n], w[:, n:2 * n], w[:, 2 * n:],
                                          DIFF_HEAD_DIM ** -0.5 * LOG2E)
            o = _diff_attention(qt, k, vt, rel_bias, b_lam[j], b_sub_g[j], lambda_init)
            h = _out_proj(o, h, b_w_out[j], g[1])
        elif m == 2:
            n = FOX_HEADS * FOX_HEAD_DIM
            w = c_w_in[j]
            qt, k, vt, kaug = _attn_projections(h, g[0], w[:, :n], w[:, n:2 * n], w[:, 2 * n:3 * n],
                                                FOX_HEAD_DIM ** -0.5 * LOG2E, fox=(w[:, 3 * n:], c_b_f[j]))
            o = _fox_attention(qt, k, kaug, vt)
            h = _out_proj(o, h, c_w_out[j], g[1])
        else:
            h = _rglru_layer(h, g[0], d_w_in[j], d_conv_w[j], d_conv_b[j], d_w_r[j], d_b_r[j],
                             d_w_i[j], d_b_i[j], d_lam[j], d_w_out[j], g[1])
        h = _conv_ffn(h, g[2], ffn_w_up[layer], ffn_conv_w[layer], ffn_conv_b[layer],
                      ffn_w_down[layer], g[3])
    return h
```

```python
import functools
import math

import numpy as np
import jax
import jax.numpy as jnp
from jax import lax
from jax.experimental import pallas as pl
from jax.experimental.pallas import tpu as pltpu

F32 = jnp.float32
BF16 = jnp.bfloat16

D_MODEL = 1024
CHUNK = 64
N_MIXERS = 4
EPS = 1e-6
GMLP_CHUNK = 128
GMLP_WIDTH = 1024
GMLP_GROUPS = 8
DIFF_HEADS = 8
DIFF_HEAD_DIM = 64
FOX_HEADS = 16
FOX_HEAD_DIM = 64
RNN_WIDTH = 1280
RNN_BLOCKS = 16
RNN_BLOCK_DIM = RNN_WIDTH // RNN_BLOCKS
RNN_CONV = 4
RGLRU_C = 8.0
D_FF = 2816
FFN_CONV = 3
REL_BUCKETS = 32
REL_MAX_DIST = 128

LANES = 128
SUBLANES = 8
MXU_TILE = 256
MASKED = -1e30
TINY = 1e-30
VMEM_LIMIT = 56 * 1024 * 1024

FFN_TILE = 512
FFN_CHUNK = 256
FFN_SLOTS = 4
ROW_TILE = 512
GMLP_TILE = 512
RNN_TILE = 256
ATT_TILE = 512
SOFTMAX_GROUP = 256
DECAY_PIECES = 3
CUMSUM_BLOCK = 128
LOG2E = math.log2(math.e)

_NT = (((1,), (1,)), ((), ()))


def _params(*sem):
    return pltpu.CompilerParams(dimension_semantics=sem, vmem_limit_bytes=VMEM_LIMIT)


def _resident(shape):
    zeros = (0,) * len(shape)
    return pl.BlockSpec(shape, lambda *_: zeros, pipeline_mode=pl.Buffered(1))


def _rows(tm, width):
    return pl.BlockSpec((None, tm, width), lambda b, s: (b, s, 0))


def _cols(height, tm):
    return pl.BlockSpec((None, height, tm), lambda b, s: (b, 0, s))


def _rms(x, g):
    return x * lax.rsqrt(jnp.mean(x * x, axis=-1, keepdims=True) + EPS) * g


def _dot(a, b):
    return jnp.dot(a, b, preferred_element_type=F32)


def _ffn_kernel(h_ref, gin_ref, wup_ref, cw_ref, cb_ref, wdn_ref, gout_ref, o_ref,
                carry_ref, acc_ref, y_ref, *bufs, tm, nc, fc):
    ext_refs = tuple((bufs[2 * i], bufs[2 * i + 1]) for i in range(FFN_SLOTS))
    act_refs = bufs[2 * FFN_SLOTS:]

    @pl.when(pl.program_id(1) == 0)
    def _():
        carry_ref[...] = jnp.zeros_like(carry_ref)

    h = h_ref[...]
    y_ref[...] = _rms(h, gin_ref[...]).astype(BF16)

    def up(c, slot):
        for branch in range(2):
            idx = c + branch * nc
            ext = ext_refs[slot][branch]
            u = _dot(y_ref[...], wup_ref[:, idx * fc:(idx + 1) * fc])
            ext[0:SUBLANES, :] = carry_ref[idx]
            ext[SUBLANES:SUBLANES + tm, :] = u
            carry_ref[idx] = u[tm - SUBLANES:tm, :]

    def activate(c, slot):
        conv = []
        for branch in range(2):
            idx = c + branch * nc
            ext = ext_refs[slot][branch]
            w = cw_ref[:, idx * fc:(idx + 1) * fc]
            conv.append(ext[pl.ds(SUBLANES, tm), :] * w[2:3] + ext[pl.ds(SUBLANES - 1, tm), :] * w[1:2]
                        + ext[pl.ds(SUBLANES - 2, tm), :] * w[0:1] + cb_ref[:, idx * fc:(idx + 1) * fc])
        act_refs[slot][...] = (jax.nn.gelu(conv[0]) * conv[1]).astype(BF16)

    def down(c, slot):
        part = _dot(act_refs[slot][...], wdn_ref[c * fc:(c + 1) * fc, :])
        if c == 0:
            acc_ref[...] = part
        else:
            acc_ref[...] += part

    lag = FFN_SLOTS - 1
    for c in range(min(lag, nc)):
        up(c, c % FFN_SLOTS)
    for c in range(nc + lag):
        if lag <= c:
            down(c - lag, (c - lag) % FFN_SLOTS)
        if c + lag < nc:
            up(c + lag, (c + lag) % FFN_SLOTS)
        if c < nc:
            activate(c, c % FFN_SLOTS)
    o_ref[...] = h + _rms(acc_ref[...], gout_ref[...])


def _conv_ffn(h, g_in, w_up, conv_w, conv_b, w_down, g_out):
    B, S, D = h.shape
    tm, fc = min(FFN_TILE, S), FFN_CHUNK
    nc = D_FF // fc
    assert D_FF % fc == 0 and S % tm == 0
    wup, cw, cb, wdn = w_up.astype(BF16), conv_w, conv_b.reshape(1, -1), w_down.astype(BF16)
    return pl.pallas_call(
        functools.partial(_ffn_kernel, tm=tm, nc=nc, fc=fc),
        out_shape=jax.ShapeDtypeStruct((B, S, D), F32),
        grid=(B, S // tm),
        in_specs=[_rows(tm, D), _resident((1, D)), _resident(wup.shape), _resident(cw.shape),
                  _resident(cb.shape), _resident(wdn.shape), _resident((1, D))],
        out_specs=_rows(tm, D),
        scratch_shapes=[pltpu.VMEM((2 * nc, SUBLANES, fc), F32),
                        pltpu.VMEM((tm, D), F32),
                        pltpu.VMEM((tm, D), BF16)]
        + [pltpu.VMEM((SUBLANES + tm, fc), F32)] * (2 * FFN_SLOTS)
        + [pltpu.VMEM((tm, fc), BF16)] * FFN_SLOTS,
        compiler_params=_params("parallel", "arbitrary"),
        name="conv_ffn",
    )(h, g_in.reshape(1, D), wup, cw, cb, wdn, g_out.reshape(1, D))


def _gmlp_kernel(h_ref, gin_ref, win_ref, lng_ref, lnb_ref, ws_ref, bs_ref, wout_ref, gout_ref,
                 o_ref, vo_ref, *, tm):
    W, G, C = GMLP_WIDTH, GMLP_GROUPS, GMLP_CHUNK
    h = h_ref[...]
    y = _rms(h, gin_ref[...]).astype(BF16)
    v_pre = _dot(y, win_ref[:, W:2 * W])
    u_pre = _dot(y, win_ref[:, 0:W])
    v = jax.nn.gelu(v_pre)
    mu = jnp.mean(v, axis=-1, keepdims=True)
    vc = v - mu
    var = jnp.mean(vc * vc, axis=-1, keepdims=True)
    vb = (vc * lax.rsqrt(var + EPS) * lng_ref[...] + lnb_ref[...]).astype(BF16)
    p_out = lax.broadcasted_iota(jnp.int32, (C, C), 0)
    q_in = lax.broadcasted_iota(jnp.int32, (C, C), 1)
    keep = (q_in // CHUNK) <= (p_out // CHUNK)
    for g in range(G):
        wsg = jnp.where(keep, ws_ref[g], 0.0).astype(BF16)
        for n in range(tm // C):
            blk = _dot(wsg, vb[n * C:(n + 1) * C, g * C:(g + 1) * C])
            vo_ref[n * C:(n + 1) * C, g * C:(g + 1) * C] = blk + bs_ref[:, g * C:(g + 1) * C]
    out = _dot((jax.nn.gelu(u_pre) * vo_ref[...]).astype(BF16), wout_ref[...])
    o_ref[...] = h + _rms(out, gout_ref[...])


def _gmlp_layer(h, g_in, w_in, ln_g, ln_b, w_s, b_s, w_out, g_out):
    B, S, D = h.shape
    tm = min(GMLP_TILE, S)
    assert S % tm == 0 and tm % GMLP_CHUNK == 0
    W = GMLP_WIDTH
    bs = jnp.repeat(b_s.T, GMLP_CHUNK, axis=1)
    return pl.pallas_call(
        functools.partial(_gmlp_kernel, tm=tm),
        out_shape=jax.ShapeDtypeStruct((B, S, D), F32),
        grid=(B, S // tm),
        in_specs=[_rows(tm, D), _resident((1, D)), _resident((D, 2 * W)), _resident((1, W)),
                  _resident((1, W)), _resident(w_s.shape), _resident(bs.shape),
                  _resident((W, D)), _resident((1, D))],
        out_specs=_rows(tm, D),
        scratch_shapes=[pltpu.VMEM((tm, W), F32)],
        compiler_params=_params("parallel", "parallel"),
        name="gmlp_mixer",
    )(h, g_in.reshape(1, D), w_in.astype(BF16), ln_g.reshape(1, W), ln_b.reshape(1, W), w_s, bs,
      w_out.astype(BF16), g_out.reshape(1, D))


def _project_qkv(h_ref, gin_ref, wqt_ref, wk_ref, wvt_ref, qt_ref, k_ref, vt_ref, scale):
    y = _rms(h_ref[...], gin_ref[...]).astype(BF16)
    qt = lax.dot_general(wqt_ref[...], y, _NT, preferred_element_type=F32)
    qt_ref[...] = (qt * scale).astype(BF16)
    k_ref[...] = _dot(y, wk_ref[...]).astype(BF16)
    vt_ref[...] = lax.dot_general(wvt_ref[...], y, _NT, preferred_element_type=F32).astype(BF16)
    return y


def _qkv_kernel(h_ref, gin_ref, wqt_ref, wk_ref, wvt_ref, qt_ref, k_ref, vt_ref, *, scale):
    _project_qkv(h_ref, gin_ref, wqt_ref, wk_ref, wvt_ref, qt_ref, k_ref, vt_ref, scale)


def _fox_proj_kernel(h_ref, gin_ref, wqt_ref, wk_ref, wvt_ref, wf_ref, bf_ref, place_ref,
                     qt_ref, k_ref, vt_ref, kaug_ref, carry_ref, *, scale, tm):
    @pl.when(pl.program_id(1) == 0)
    def _():
        carry_ref[...] = jnp.zeros_like(carry_ref)

    y = _project_qkv(h_ref, gin_ref, wqt_ref, wk_ref, wvt_ref, qt_ref, k_ref, vt_ref, scale)
    f = _dot(y, wf_ref[...]) + bf_ref[...]
    logf = jnp.minimum(f, 0.0) - jnp.log(1.0 + jnp.exp(-jnp.abs(f)))
    n = CUMSUM_BLOCK
    row = lax.broadcasted_iota(jnp.int32, (n, n), 0)
    col = lax.broadcasted_iota(jnp.int32, (n, n), 1)
    tri = jnp.where(col <= row, 1.0, 0.0).astype(BF16)
    total = carry_ref[0:1, :]
    for r0 in range(0, tm, n):
        block = _dot_pieces(tri, logf[r0:r0 + n, :]) + total
        kaug_ref[r0:r0 + n, :] = _dot_pieces(block * (-LOG2E), place_ref, left=True).astype(BF16)
        total = block[n - 1:n, :]
    carry_ref[...] = jnp.broadcast_to(total, carry_ref.shape)


def _dot_pieces(a, b, left=False):
    rest = a if left else b
    out = None
    for c in range(DECAY_PIECES):
        piece = rest.astype(BF16)
        rest = rest - piece.astype(F32)
        term = _dot(piece, b[c]) if left else _dot(a, piece)
        out = term if out is None else out + term
    return out


def _decay_placement(n_heads):
    assert DECAY_PIECES * n_heads <= LANES
    place = np.zeros((DECAY_PIECES, LANES, LANES), np.float32)
    for h in range(n_heads):
        for c in range(DECAY_PIECES):
            place[c, h, DECAY_PIECES * h + c] = 1.0
    return jnp.asarray(place, BF16)


def _attn_projections(h, g_in, wq, wk, wv, scale, fox=None):
    B, S, D = h.shape
    tm = min(ROW_TILE, S)
    assert S % tm == 0
    nq, nk, nv = wq.shape[1], wk.shape[1], wv.shape[1]
    in_specs = [_rows(tm, D), _resident((1, D)), _resident((nq, D)), _resident((D, nk)),
                _resident((nv, D))]
    out_shape = [jax.ShapeDtypeStruct((B, nq, S), BF16), jax.ShapeDtypeStruct((B, S, nk), BF16),
                 jax.ShapeDtypeStruct((B, nv, S), BF16)]
    out_specs = [_cols(nq, tm), _rows(tm, nk), _cols(nv, tm)]
    args = [h, g_in.reshape(1, D), wq.T.astype(BF16), wk.astype(BF16), wv.T.astype(BF16)]
    if fox is None:
        return pl.pallas_call(
            functools.partial(_qkv_kernel, scale=scale),
            out_shape=out_shape, grid=(B, S // tm), in_specs=in_specs, out_specs=out_specs,
            compiler_params=_params("parallel", "parallel"), name="qkv_proj",
        )(*args)
    wf, bf = fox
    H = wf.shape[1]
    wf_pad = jnp.zeros((D, LANES), F32).at[:, :H].set(wf).astype(BF16)
    bf_pad = jnp.zeros((1, LANES), F32).at[0, :H].set(bf)
    place = _decay_placement(H)
    in_specs += [_resident((D, LANES)), _resident((1, LANES)), _resident(place.shape)]
    out_shape += [jax.ShapeDtypeStruct((B, S, place.shape[2]), BF16)]
    out_specs += [_rows(tm, place.shape[2])]
    return pl.pallas_call(
        functools.partial(_fox_proj_kernel, scale=scale, tm=tm),
        out_shape=out_shape, grid=(B, S // tm), in_specs=in_specs, out_specs=out_specs,
        scratch_shapes=[pltpu.VMEM((SUBLANES, LANES), F32)],
        compiler_params=_params("parallel", "arbitrary"), name="fox_proj",
    )(*args, wf_pad, bf_pad, place)


def _out_proj_kernel(o_ref, h_ref, w_ref, g_ref, out_ref):
    out_ref[...] = h_ref[...] + _rms(_dot(o_ref[...], w_ref[...]), g_ref[...])


def _out_proj(o, h, w_out, g_out):
    B, S, D = h.shape
    n = o.shape[-1]
    tm = min(ROW_TILE, S)
    return pl.pallas_call(
        _out_proj_kernel,
        out_shape=jax.ShapeDtypeStruct((B, S, D), F32),
        grid=(B, S // tm),
        in_specs=[_rows(tm, n), _rows(tm, D), _resident((n, D)), _resident((1, D))],
        out_specs=_rows(tm, D),
        compiler_params=_params("parallel", "parallel"), name="out_proj",
    )(o, h, w_out.astype(BF16), g_out.reshape(1, D))


def _softmax_reset(m_ref, l_ref, acc_ref):
    m_ref[...] = jnp.full(m_ref.shape, MASKED, F32)
    l_ref[...] = jnp.zeros_like(l_ref)
    acc_ref[...] = jnp.zeros_like(acc_ref)


def _score_tile(keys, qa_ref, s_refs):
    group = qa_ref.shape[1] // len(s_refs)
    for g, s_ref in enumerate(s_refs):
        s_ref[...] = _dot(keys, qa_ref[:, g * group:(g + 1) * group])


def _consume_tile(vt, next_keys, qa_ref, s_refs, m_refs, l_refs, acc_refs, adjust=None):
    group = qa_ref.shape[1] // len(s_refs)
    pending = None

    def accumulate(g, alpha, p):
        rows = acc_refs[g].shape[0]
        r0 = 0 if rows == vt.shape[0] else (2 * g * group // qa_ref.shape[1]) * rows
        acc_refs[g][...] = alpha * acc_refs[g][...] + _dot(vt[r0:r0 + rows, :], p)

    for g in range(len(s_refs)):
        s = s_refs[g][...]
        if adjust is not None:
            s = adjust(s, g * group)
        m_old = m_refs[g][...]
        m_new = jnp.maximum(m_old, jnp.max(s, axis=0, keepdims=True))
        alpha = jnp.exp2(m_old - m_new)
        p = jnp.exp2(s - m_new)
        l_refs[g][...] = alpha * l_refs[g][...] + jnp.sum(p, axis=0, keepdims=True)
        m_refs[g][...] = m_new
        if next_keys is not None:
            s_refs[g][...] = _dot(next_keys, qa_ref[:, g * group:(g + 1) * group])
        if pending is not None:
            accumulate(*pending)
        pending = (g, alpha, p.astype(BF16))
    accumulate(*pending)


def _pipelined_tiles(n_far, keys_of, vt_of, qa_ref, s_refs, m_refs, l_refs, acc_refs, last_adjust,
                     prev_adjust=None):
    state = (qa_ref, s_refs, m_refs, l_refs, acc_refs)
    _score_tile(keys_of(0), qa_ref, s_refs)

    def far(j):
        _consume_tile(vt_of(j), keys_of(j + 1), *state)

    def far_pair(i, carry):
        far(2 * i)
        far(2 * i + 1)
        return carry

    n_plain = n_far if prev_adjust is None else jnp.maximum(n_far - 1, 0)
    lax.fori_loop(0, n_plain // 2, far_pair, 0)

    @pl.when(n_plain % 2 == 1)
    def _():
        far(n_plain - 1)

    if prev_adjust is not None:

        @pl.when(n_far > 0)
        def _():
            _consume_tile(vt_of(n_far - 1), keys_of(n_far), *state, adjust=prev_adjust)
    _consume_tile(vt_of(n_far), None, *state, adjust=last_adjust)


def _attention_state(t, value_dim):
    groups = 2 * t // SOFTMAX_GROUP
    shapes = ((t, SOFTMAX_GROUP), (1, SOFTMAX_GROUP), (1, SOFTMAX_GROUP), (value_dim, SOFTMAX_GROUP))
    return [pltpu.VMEM(shape, F32) for shape in shapes for _ in range(groups)]


def _split_rows(qt_ref, qa_ref, d, t):
    qt = qt_ref[...]
    row = lax.broadcasted_iota(jnp.int32, qt.shape, 0)
    zero = jnp.zeros_like(qt)
    qa_ref[0:2 * d, 0:t] = jnp.where(row < d, qt, zero)
    qa_ref[0:2 * d, t:2 * t] = jnp.where(row >= d, qt, zero)


def _t5_bucket(rel):
    half = REL_BUCKETS // 2
    max_exact = half // 2
    n = jnp.abs(rel)
    ret = jnp.where(rel > 0, half, 0)
    nf = jnp.maximum(n, 1).astype(jnp.float32)
    large = max_exact + (jnp.log(nf / max_exact) / math.log(REL_MAX_DIST / max_exact)
                         * (half - max_exact)).astype(jnp.int32)
    large = jnp.minimum(large, half - 1)
    return ret + jnp.where(n < max_exact, n, large)


def _diff_bias_tiles(rel_bias, t):
    assert t >= REL_MAX_DIST
    kpos = jnp.arange(t)[:, None]
    qpos = jnp.arange(t)[None, :]
    rel = jnp.stack([kpos - qpos - t, kpos - qpos], axis=0)
    bucket = _t5_bucket(rel)
    far = rel_bias[_t5_bucket(jnp.int32(-REL_MAX_DIST))]
    bias = jnp.zeros((rel_bias.shape[1],) + bucket.shape, F32)
    for b in range(REL_BUCKETS):
        bias = jnp.where(bucket == b, (rel_bias[b] - far)[:, None, None, None], bias)
    keep = jnp.stack([jnp.ones((t, t), bool), (kpos // CHUNK) <= (qpos // CHUNK)], axis=0)
    return jnp.where(keep, bias * LOG2E, MASKED)


def _diff_attn_kernel(qt_ref, k_ref, vt_ref, bias_ref, lam_ref, subg_ref, o_ref, qa_ref, *state,
                      t, lambda_init):
    qi = pl.program_id(2)
    G = 2 * t // SOFTMAX_GROUP
    s_refs, m_refs, l_refs, acc_refs = (state[i * G:(i + 1) * G] for i in range(4))
    _split_rows(qt_ref, qa_ref, DIFF_HEAD_DIM, t)
    for g in range(G):
        _softmax_reset(m_refs[g], l_refs[g], acc_refs[g])

    def add_bias(tile):
        def adjust(s, c0):
            return s + bias_ref[tile, :, c0 % t:c0 % t + s.shape[1]]
        return adjust

    _pipelined_tiles(qi, lambda j: k_ref[pl.ds(pl.multiple_of(j * t, t), t), :],
                     lambda j: vt_ref[:, pl.ds(pl.multiple_of(j * t, t), t)], qa_ref,
                     s_refs, m_refs, l_refs, acc_refs, last_adjust=add_bias(1), prev_adjust=add_bias(0))

    lam = lam_ref[...]
    lam_full = (jnp.exp(jnp.sum(lam[0:1] * lam[1:2], axis=-1, keepdims=True))
                - jnp.exp(jnp.sum(lam[2:3] * lam[3:4], axis=-1, keepdims=True)) + lambda_init)
    o = jnp.concatenate([acc_refs[g][...] / l_refs[g][...] for g in range(G)], axis=1)
    o = (o[:, 0:t] - lam_full * o[:, t:2 * t]).T
    o_ref[...] = (_rms(o, subg_ref[...]) * (1.0 - lambda_init)).astype(BF16)


def _diff_attention(qt, k, vt, rel_bias, lam, sub_g, lambda_init):
    B, S, _ = k.shape
    t = min(ATT_TILE, S)
    assert S % t == 0
    H, W = DIFF_HEADS, 2 * DIFF_HEAD_DIM
    bias = _diff_bias_tiles(rel_bias, t)
    return pl.pallas_call(
        functools.partial(_diff_attn_kernel, t=t, lambda_init=lambda_init),
        out_shape=jax.ShapeDtypeStruct((B, S, H * W), BF16),
        grid=(B, H, S // t),
        in_specs=[pl.BlockSpec((None, W, t), lambda b, h, i: (b, h, i)),
                  pl.BlockSpec((None, S, W), lambda b, h, i: (b, 0, h)),
                  pl.BlockSpec((None, W, S), lambda b, h, i: (b, h, 0)),
                  pl.BlockSpec((None, 2, t, t), lambda b, h, i: (h, 0, 0, 0)),
                  pl.BlockSpec((4, DIFF_HEAD_DIM), lambda b, h, i: (0, 0)),
                  pl.BlockSpec((1, W), lambda b, h, i: (0, 0))],
        out_specs=pl.BlockSpec((None, t, W), lambda b, h, i: (b, i, h)),
        scratch_shapes=[pltpu.VMEM((W, 2 * t), BF16)] + _attention_state(t, W),
        compiler_params=_params("parallel", "parallel", "arbitrary"), name="diff_attention",
    )(qt, k, vt, bias, lam, sub_g.reshape(1, W))


def _fox_attn_kernel(qt_ref, k_ref, kaug_ref, vt_ref, o_ref, qa_ref, *state, t):
    qi = pl.program_id(2)
    d = FOX_HEAD_DIM
    G = 2 * t // SOFTMAX_GROUP
    s_refs, m_refs, l_refs, acc_refs = (state[i * G:(i + 1) * G] for i in range(4))
    _split_rows(qt_ref, qa_ref, d, t)
    r = lax.broadcasted_iota(jnp.int32, (LANES, 2 * t), 0)
    c = lax.broadcasted_iota(jnp.int32, (LANES, 2 * t), 1)
    first = DECAY_PIECES * (2 * pl.program_id(1) + jnp.where(c < t, 0, 1))
    qa_ref[2 * d:, :] = jnp.where(r < first, 0.0, jnp.where(r < first + DECAY_PIECES, 1.0, 0.0)
                                  ).astype(BF16)
    for g in range(G):
        _softmax_reset(m_refs[g], l_refs[g], acc_refs[g])

    def keys_of(j):
        k0 = pl.multiple_of(j * t, t)
        return jnp.concatenate([k_ref[pl.ds(k0, t), :], kaug_ref[pl.ds(k0, t), :]], axis=1)

    def causal(s, c0):
        key = lax.broadcasted_iota(jnp.int32, s.shape, 0)
        qry = (lax.broadcasted_iota(jnp.int32, s.shape, 1) + c0) & (t - 1)
        return jnp.where(key <= qry, s, MASKED)

    _pipelined_tiles(qi, keys_of, lambda j: vt_ref[:, pl.ds(pl.multiple_of(j * t, t), t)], qa_ref,
                     s_refs, m_refs, l_refs, acc_refs, last_adjust=causal)
    o = jnp.concatenate([acc_refs[g][...] / l_refs[g][...] for g in range(G)], axis=1)
    o = jnp.concatenate([o[:, 0:t], o[:, t:2 * t]], axis=0)
    o_ref[...] = o.T.astype(BF16)


def _fox_attention(qt, k, kaug, vt):
    B, S, _ = k.shape
    t = min(ATT_TILE, S)
    assert S % t == 0 and (t & (t - 1)) == 0
    H, d = FOX_HEADS, FOX_HEAD_DIM
    W = 2 * d
    return pl.pallas_call(
        functools.partial(_fox_attn_kernel, t=t),
        out_shape=jax.ShapeDtypeStruct((B, S, H * d), BF16),
        grid=(B, H // 2, S // t),
        in_specs=[pl.BlockSpec((None, W, t), lambda b, p, i: (b, p, i)),
                  pl.BlockSpec((None, S, W), lambda b, p, i: (b, 0, p)),
                  pl.BlockSpec((None, S, LANES), lambda b, p, i: (b, 0, 0)),
                  pl.BlockSpec((None, W, S), lambda b, p, i: (b, p, 0))],
        out_specs=pl.BlockSpec((None, t, W), lambda b, p, i: (b, i, p)),
        scratch_shapes=[pltpu.VMEM((W + LANES, 2 * t), BF16)] + _attention_state(t, d),
        compiler_params=_params("parallel", "parallel", "arbitrary"), name="fox_attention",
    )(qt, k, kaug, vt)


def _block_diag_dot(x, w_ref):
    n, bd, tile = w_ref.shape[1], RNN_BLOCK_DIM, MXU_TILE
    outs = []
    for c0 in range(0, n, tile):
        c1 = min(c0 + tile, n)
        k0 = (c0 // bd) * bd // tile * tile
        k1 = min(-(-(-(-c1 // bd) * bd) // tile) * tile, n)
        outs.append(_dot(x[:, k0:k1], w_ref[k0:k1, c0:c1]))
    return jnp.concatenate(outs, axis=1)


def _rglru_kernel(h_ref, gin_ref, win_ref, cw_ref, cb_ref, wr_ref, br_ref, wi_ref, bi_ref, lam_ref,
                  wout_ref, gout_ref, o_ref, ext_ref, xcarry_ref, hcarry_ref, hs_ref, *, tm):
    R = RNN_WIDTH

    @pl.when(pl.program_id(1) == 0)
    def _():
        xcarry_ref[...] = jnp.zeros_like(xcarry_ref)
        hcarry_ref[...] = jnp.zeros_like(hcarry_ref)

    h_in = h_ref[...]
    y = _rms(h_in, gin_ref[...]).astype(BF16)
    x = _dot(y, win_ref[:, R:2 * R])
    ext_ref[0:SUBLANES, :] = xcarry_ref[...]
    ext_ref[SUBLANES:SUBLANES + tm, :] = x
    xcarry_ref[...] = x[tm - SUBLANES:tm, :]
    cw = cw_ref[...]
    xr = x * cw[3:4] + cb_ref[...]
    for k in range(1, RNN_CONV):
        xr = xr + ext_ref[pl.ds(SUBLANES - k, tm), :] * cw[RNN_CONV - 1 - k:RNN_CONV - k]
    xb = xr.astype(BF16)
    r = jax.nn.sigmoid(_block_diag_dot(xb, wr_ref) + br_ref[...])
    i = jax.nn.sigmoid(_block_diag_dot(xb, wi_ref) + bi_ref[...])
    lam = lam_ref[...]
    softplus_neg = jnp.maximum(-lam, 0.0) + jnp.log(1.0 + jnp.exp(-jnp.abs(lam)))
    a = jnp.exp((-RGLRU_C) * r * softplus_neg)
    one_minus_a2 = 1.0 - a * a
    u = one_minus_a2 * lax.rsqrt(jnp.maximum(one_minus_a2, TINY)) * (i * xr)

    row = lax.broadcasted_iota(jnp.int32, (tm, R), 0) % SUBLANES
    shift = 1
    while shift < SUBLANES:
        live = row >= shift
        a_prev = jnp.where(live, pltpu.roll(a, shift, axis=0), 1.0)
        u_prev = jnp.where(live, pltpu.roll(u, shift, axis=0), 0.0)
        u = a * u_prev + u
        a = a * a_prev
        shift *= 2
    carry = hcarry_ref[...]
    for g in range(tm // SUBLANES):
        rows = slice(g * SUBLANES, (g + 1) * SUBLANES)
        hg = a[rows] * carry + u[rows]
        hs_ref[rows, :] = hg
        carry = jnp.broadcast_to(hg[SUBLANES - 1:SUBLANES, :], (SUBLANES, R))
    hcarry_ref[...] = carry
    hs = hs_ref[...]

    gate = jax.nn.gelu(_dot(y, win_ref[:, 0:R]))
    out = _dot((hs * gate).astype(BF16), wout_ref[...])
    o_ref[...] = h_in + _rms(out, gout_ref[...])


def _block_diag(w):
    n, c, d = w.shape
    eye = jnp.eye(n, dtype=w.dtype)
    return (w[:, :, None, :] * eye[:, None, :, None]).reshape(n * c, n * d)


def _rglru_layer(h, g_in, w_in, conv_w, conv_b, w_r, b_r, w_i, b_i, lam, w_out, g_out):
    B, S, D = h.shape
    tm = min(RNN_TILE, S)
    assert S % tm == 0
    R = RNN_WIDTH
    vec = lambda a: a.reshape(1, -1)
    return pl.pallas_call(
        functools.partial(_rglru_kernel, tm=tm),
        out_shape=jax.ShapeDtypeStruct((B, S, D), F32),
        grid=(B, S // tm),
        in_specs=[_rows(tm, D), _resident((1, D)), _resident((D, 2 * R)), _resident((RNN_CONV, R)),
                  _resident((1, R)), _resident((R, R)), _resident((1, R)), _resident((R, R)),
                  _resident((1, R)), _resident((1, R)), _resident((R, D)), _resident((1, D))],
        out_specs=_rows(tm, D),
        scratch_shapes=[pltpu.VMEM((SUBLANES + tm, R), F32), pltpu.VMEM((SUBLANES, R), F32),
                        pltpu.VMEM((SUBLANES, R), F32), pltpu.VMEM((tm, R), F32)],
        compiler_params=_params("parallel", "arbitrary"), name="rglru_mixer",
    )(h, vec(g_in), w_in.astype(BF16), conv_w, vec(conv_b), _block_diag(w_r).astype(BF16), vec(b_r),
      _block_diag(w_i).astype(BF16), vec(b_i), vec(lam), w_out.astype(BF16), vec(g_out))


def kernel(x, norm_g, ffn_w_up, ffn_conv_w, ffn_conv_b, ffn_w_down, rel_bias, a_w_in, a_ln_g, a_ln_b, a_w_s, a_b_s, a_w_out, b_w_in, b_lam, b_sub_g, b_w_out, c_w_in, c_b_f, c_w_out, d_w_in, d_conv_w, d_conv_b, d_w_r, d_b_r, d_w_i, d_b_i, d_lam, d_w_out):
    depth = norm_g.shape[0]
    h = x
    for layer in range(depth):
        m = layer % N_MIXERS
        j = layer // N_MIXERS
        g = norm_g[layer]
        if m == 0:
            h = _gmlp_layer(h, g[0], a_w_in[j], a_ln_g[j], a_ln_b[j], a_w_s[j], a_b_s[j], a_w_out[j], g[1])
        elif m == 1:
            lambda_init = 0.8 - 0.6 * math.exp(-0.3 * layer)
            n = DIFF_HEADS * 2 * DIFF_HEAD_DIM
            w = b_w_in[j]
            qt, k, vt = _attn_projections(h, g[0], w[:, :n], w[:, n:2 * n], w[:, 2 * n:],
                                          DIFF_HEAD_DIM ** -0.5 * LOG2E)
            o = _diff_attention(qt, k, vt, rel_bias, b_lam[j], b_sub_g[j], lambda_init)
            h = _out_proj(o, h, b_w_out[j], g[1])
        elif m == 2:
            n = FOX_HEADS * FOX_HEAD_DIM
            w = c_w_in[j]
            qt, k, vt, kaug = _attn_projections(h, g[0], w[:, :n], w[:, n:2 * n], w[:, 2 * n:3 * n],
                                                FOX_HEAD_DIM ** -0.5 * LOG2E, fox=(w[:, 3 * n:], c_b_f[j]))
            o = _fox_attention(qt, k, kaug, vt)
            h = _out_proj(o, h, c_w_out[j], g[1])
        else:
            h = _rglru_layer(h, g[0], d_w_in[j], d_conv_w[j], d_conv_b[j], d_w_r[j], d_b_r[j],
                             d_w_i[j], d_b_i[j], d_lam[j], d_w_out[j], g[1])
        h = _conv_ffn(h, g[2], ffn_w_up[layer], ffn_conv_w[layer], ffn_conv_b[layer],
                      ffn_w_down[layer], g[3])
    return h
```

```python
import functools
import math

import numpy as np
import jax
import jax.numpy as jnp
from jax import lax
from jax.experimental import pallas as pl
from jax.experimental.pallas import tpu as pltpu

F32 = jnp.float32
BF16 = jnp.bfloat16

D_MODEL = 1024
CHUNK = 64
N_MIXERS = 4
EPS = 1e-6
GMLP_CHUNK = 128
GMLP_WIDTH = 1024
GMLP_GROUPS = 8
DIFF_HEADS = 8
DIFF_HEAD_DIM = 64
FOX_HEADS = 16
FOX_HEAD_DIM = 64
RNN_WIDTH = 1280
RNN_BLOCKS = 16
RNN_BLOCK_DIM = RNN_WIDTH // RNN_BLOCKS
RNN_CONV = 4
RGLRU_C = 8.0
D_FF = 2816
FFN_CONV = 3
REL_BUCKETS = 32
REL_MAX_DIST = 128

LANES = 128
SUBLANES = 8
MXU_TILE = 256
MASKED = -1e30
TINY = 1e-30
VMEM_LIMIT = 56 * 1024 * 1024

FFN_TILE = 512
FFN_CHUNK = 256
FFN_SLOTS = 4
ROW_TILE = 512
GMLP_TILE = 512
RNN_TILE = 256
ATT_TILE = 512
SOFTMAX_GROUP = 256
SUM_ROWS = 16
DECAY_PIECES = 3
CUMSUM_BLOCK = 128
LOG2E = math.log2(math.e)

_NT = (((1,), (1,)), ((), ()))


def _params(*sem):
    return pltpu.CompilerParams(dimension_semantics=sem, vmem_limit_bytes=VMEM_LIMIT)


def _resident(shape):
    zeros = (0,) * len(shape)
    return pl.BlockSpec(shape, lambda *_: zeros, pipeline_mode=pl.Buffered(1))


def _rows(tm, width):
    return pl.BlockSpec((None, tm, width), lambda b, s: (b, s, 0))


def _cols(height, tm):
    return pl.BlockSpec((None, height, tm), lambda b, s: (b, 0, s))


def _rms(x, g):
    return x * lax.rsqrt(jnp.mean(x * x, axis=-1, keepdims=True) + EPS) * g


def _dot(a, b):
    return jnp.dot(a, b, preferred_element_type=F32)


def _ffn_kernel(h_ref, gin_ref, wup_ref, cw_ref, cb_ref, wdn_ref, gout_ref, o_ref,
                carry_ref, acc_ref, y_ref, *bufs, tm, nc, fc):
    ext_refs = tuple((bufs[2 * i], bufs[2 * i + 1]) for i in range(FFN_SLOTS))
    act_refs = bufs[2 * FFN_SLOTS:]

    @pl.when(pl.program_id(1) == 0)
    def _():
        carry_ref[...] = jnp.zeros_like(carry_ref)

    h = h_ref[...]
    y_ref[...] = _rms(h, gin_ref[...]).astype(BF16)

    def up(c, slot):
        for branch in range(2):
            idx = c + branch * nc
            ext = ext_refs[slot][branch]
            u = _dot(y_ref[...], wup_ref[:, idx * fc:(idx + 1) * fc])
            ext[0:SUBLANES, :] = carry_ref[idx]
            ext[SUBLANES:SUBLANES + tm, :] = u
            carry_ref[idx] = u[tm - SUBLANES:tm, :]

    def activate(c, slot):
        conv = []
        for branch in range(2):
            idx = c + branch * nc
            ext = ext_refs[slot][branch]
            w = cw_ref[:, idx * fc:(idx + 1) * fc]
            conv.append(ext[pl.ds(SUBLANES, tm), :] * w[2:3] + ext[pl.ds(SUBLANES - 1, tm), :] * w[1:2]
                        + ext[pl.ds(SUBLANES - 2, tm), :] * w[0:1] + cb_ref[:, idx * fc:(idx + 1) * fc])
        act_refs[slot][...] = (jax.nn.gelu(conv[0]) * conv[1]).astype(BF16)

    def down(c, slot):
        part = _dot(act_refs[slot][...], wdn_ref[c * fc:(c + 1) * fc, :])
        if c == 0:
            acc_ref[...] = part
        else:
            acc_ref[...] += part

    lag = FFN_SLOTS - 1
    for c in range(min(lag, nc)):
        up(c, c % FFN_SLOTS)
    for c in range(nc + lag):
        if lag <= c:
            down(c - lag, (c - lag) % FFN_SLOTS)
        if c + lag < nc:
            up(c + lag, (c + lag) % FFN_SLOTS)
        if c < nc:
            activate(c, c % FFN_SLOTS)
    o_ref[...] = h + _rms(acc_ref[...], gout_ref[...])


def _conv_ffn(h, g_in, w_up, conv_w, conv_b, w_down, g_out):
    B, S, D = h.shape
    tm, fc = min(FFN_TILE, S), FFN_CHUNK
    nc = D_FF // fc
    assert D_FF % fc == 0 and S % tm == 0
    wup, cw, cb, wdn = w_up.astype(BF16), conv_w, conv_b.reshape(1, -1), w_down.astype(BF16)
    return pl.pallas_call(
        functools.partial(_ffn_kernel, tm=tm, nc=nc, fc=fc),
        out_shape=jax.ShapeDtypeStruct((B, S, D), F32),
        grid=(B, S // tm),
        in_specs=[_rows(tm, D), _resident((1, D)), _resident(wup.shape), _resident(cw.shape),
                  _resident(cb.shape), _resident(wdn.shape), _resident((1, D))],
        out_specs=_rows(tm, D),
        scratch_shapes=[pltpu.VMEM((2 * nc, SUBLANES, fc), F32),
                        pltpu.VMEM((tm, D), F32),
                        pltpu.VMEM((tm, D), BF16)]
        + [pltpu.VMEM((SUBLANES + tm, fc), F32)] * (2 * FFN_SLOTS)
        + [pltpu.VMEM((tm, fc), BF16)] * FFN_SLOTS,
        compiler_params=_params("parallel", "arbitrary"),
        name="conv_ffn",
    )(h, g_in.reshape(1, D), wup, cw, cb, wdn, g_out.reshape(1, D))


def _gmlp_kernel(h_ref, gin_ref, win_ref, lng_ref, lnb_ref, ws_ref, bs_ref, wout_ref, gout_ref,
                 o_ref, vo_ref, *, tm):
    W, G, C = GMLP_WIDTH, GMLP_GROUPS, GMLP_CHUNK
    h = h_ref[...]
    y = _rms(h, gin_ref[...]).astype(BF16)
    v_pre = _dot(y, win_ref[:, W:2 * W])
    u_pre = _dot(y, win_ref[:, 0:W])
    v = jax.nn.gelu(v_pre)
    mu = jnp.mean(v, axis=-1, keepdims=True)
    vc = v - mu
    var = jnp.mean(vc * vc, axis=-1, keepdims=True)
    vb = (vc * lax.rsqrt(var + EPS) * lng_ref[...] + lnb_ref[...]).astype(BF16)
    p_out = lax.broadcasted_iota(jnp.int32, (C, C), 0)
    q_in = lax.broadcasted_iota(jnp.int32, (C, C), 1)
    keep = (q_in // CHUNK) <= (p_out // CHUNK)
    for g in range(G):
        wsg = jnp.where(keep, ws_ref[g], 0.0).astype(BF16)
        for n in range(tm // C):
            blk = _dot(wsg, vb[n * C:(n + 1) * C, g * C:(g + 1) * C])
            vo_ref[n * C:(n + 1) * C, g * C:(g + 1) * C] = blk + bs_ref[:, g * C:(g + 1) * C]
    out = _dot((jax.nn.gelu(u_pre) * vo_ref[...]).astype(BF16), wout_ref[...])
    o_ref[...] = h + _rms(out, gout_ref[...])


def _gmlp_layer(h, g_in, w_in, ln_g, ln_b, w_s, b_s, w_out, g_out):
    B, S, D = h.shape
    tm = min(GMLP_TILE, S)
    assert S % tm == 0 and tm % GMLP_CHUNK == 0
    W = GMLP_WIDTH
    bs = jnp.repeat(b_s.T, GMLP_CHUNK, axis=1)
    return pl.pallas_call(
        functools.partial(_gmlp_kernel, tm=tm),
        out_shape=jax.ShapeDtypeStruct((B, S, D), F32),
        grid=(B, S // tm),
        in_specs=[_rows(tm, D), _resident((1, D)), _resident((D, 2 * W)), _resident((1, W)),
                  _resident((1, W)), _resident(w_s.shape), _resident(bs.shape),
                  _resident((W, D)), _resident((1, D))],
        out_specs=_rows(tm, D),
        scratch_shapes=[pltpu.VMEM((tm, W), F32)],
        compiler_params=_params("parallel", "parallel"),
        name="gmlp_mixer",
    )(h, g_in.reshape(1, D), w_in.astype(BF16), ln_g.reshape(1, W), ln_b.reshape(1, W), w_s, bs,
      w_out.astype(BF16), g_out.reshape(1, D))


def _project_qkv(h_ref, gin_ref, wqt_ref, wk_ref, wvt_ref, qt_ref, k_ref, vt_ref, scale):
    y = _rms(h_ref[...], gin_ref[...]).astype(BF16)
    qt = lax.dot_general(wqt_ref[...], y, _NT, preferred_element_type=F32)
    qt_ref[...] = (qt * scale).astype(BF16)
    k_ref[...] = _dot(y, wk_ref[...]).astype(BF16)
    vt_ref[...] = lax.dot_general(wvt_ref[...], y, _NT, preferred_element_type=F32).astype(BF16)
    return y


def _qkv_kernel(h_ref, gin_ref, wqt_ref, wk_ref, wvt_ref, qt_ref, k_ref, vt_ref, *, scale):
    _project_qkv(h_ref, gin_ref, wqt_ref, wk_ref, wvt_ref, qt_ref, k_ref, vt_ref, scale)


def _fox_proj_kernel(h_ref, gin_ref, wqt_ref, wk_ref, wvt_ref, wf_ref, bf_ref, place_ref,
                     qt_ref, k_ref, vt_ref, kaug_ref, carry_ref, *, scale, tm):
    @pl.when(pl.program_id(1) == 0)
    def _():
        carry_ref[...] = jnp.zeros_like(carry_ref)

    y = _project_qkv(h_ref, gin_ref, wqt_ref, wk_ref, wvt_ref, qt_ref, k_ref, vt_ref, scale)
    f = _dot(y, wf_ref[...]) + bf_ref[...]
    logf = jnp.minimum(f, 0.0) - jnp.log(1.0 + jnp.exp(-jnp.abs(f)))
    n = CUMSUM_BLOCK
    row = lax.broadcasted_iota(jnp.int32, (n, n), 0)
    col = lax.broadcasted_iota(jnp.int32, (n, n), 1)
    tri = jnp.where(col <= row, 1.0, 0.0).astype(BF16)
    total = carry_ref[0:1, :]
    for r0 in range(0, tm, n):
        block = _dot_pieces(tri, logf[r0:r0 + n, :]) + total
        kaug_ref[r0:r0 + n, :] = _dot_pieces(block * (-LOG2E), place_ref, left=True).astype(BF16)
        total = block[n - 1:n, :]
    carry_ref[...] = jnp.broadcast_to(total, carry_ref.shape)


def _dot_pieces(a, b, left=False):
    rest = a if left else b
    out = None
    for c in range(DECAY_PIECES):
        piece = rest.astype(BF16)
        rest = rest - piece.astype(F32)
        term = _dot(piece, b[c]) if left else _dot(a, piece)
        out = term if out is None else out + term
    return out


def _decay_placement(n_heads):
    assert DECAY_PIECES * n_heads <= LANES
    place = np.zeros((DECAY_PIECES, LANES, LANES), np.float32)
    for h in range(n_heads):
        for c in range(DECAY_PIECES):
            place[c, h, DECAY_PIECES * h + c] = 1.0
    return jnp.asarray(place, BF16)


def _attn_projections(h, g_in, wq, wk, wv, scale, fox=None):
    B, S, D = h.shape
    tm = min(ROW_TILE, S)
    assert S % tm == 0
    nq, nk, nv = wq.shape[1], wk.shape[1], wv.shape[1]
    in_specs = [_rows(tm, D), _resident((1, D)), _resident((nq, D)), _resident((D, nk)),
                _resident((nv, D))]
    out_shape = [jax.ShapeDtypeStruct((B, nq, S), BF16), jax.ShapeDtypeStruct((B, S, nk), BF16),
                 jax.ShapeDtypeStruct((B, nv, S), BF16)]
    out_specs = [_cols(nq, tm), _rows(tm, nk), _cols(nv, tm)]
    args = [h, g_in.reshape(1, D), wq.T.astype(BF16), wk.astype(BF16), wv.T.astype(BF16)]
    if fox is None:
        return pl.pallas_call(
            functools.partial(_qkv_kernel, scale=scale),
            out_shape=out_shape, grid=(B, S // tm), in_specs=in_specs, out_specs=out_specs,
            compiler_params=_params("parallel", "parallel"), name="qkv_proj",
        )(*args)
    wf, bf = fox
    H = wf.shape[1]
    wf_pad = jnp.zeros((D, LANES), F32).at[:, :H].set(wf).astype(BF16)
    bf_pad = jnp.zeros((1, LANES), F32).at[0, :H].set(bf)
    place = _decay_placement(H)
    in_specs += [_resident((D, LANES)), _resident((1, LANES)), _resident(place.shape)]
    out_shape += [jax.ShapeDtypeStruct((B, S, place.shape[2]), BF16)]
    out_specs += [_rows(tm, place.shape[2])]
    return pl.pallas_call(
        functools.partial(_fox_proj_kernel, scale=scale, tm=tm),
        out_shape=out_shape, grid=(B, S // tm), in_specs=in_specs, out_specs=out_specs,
        scratch_shapes=[pltpu.VMEM((SUBLANES, LANES), F32)],
        compiler_params=_params("parallel", "arbitrary"), name="fox_proj",
    )(*args, wf_pad, bf_pad, place)


def _out_proj_kernel(o_ref, h_ref, w_ref, g_ref, out_ref):
    out_ref[...] = h_ref[...] + _rms(_dot(o_ref[...], w_ref[...]), g_ref[...])


def _out_proj(o, h, w_out, g_out):
    B, S, D = h.shape
    n = o.shape[-1]
    tm = min(ROW_TILE, S)
    return pl.pallas_call(
        _out_proj_kernel,
        out_shape=jax.ShapeDtypeStruct((B, S, D), F32),
        grid=(B, S // tm),
        in_specs=[_rows(tm, n), _rows(tm, D), _resident((n, D)), _resident((1, D))],
        out_specs=_rows(tm, D),
        compiler_params=_params("parallel", "parallel"), name="out_proj",
    )(o, h, w_out.astype(BF16), g_out.reshape(1, D))


def _softmax_reset(m_ref, acc_ref):
    m_ref[...] = jnp.full(m_ref.shape, MASKED, F32)
    acc_ref[...] = jnp.zeros_like(acc_ref)


def _consume_tile(vt, s_refs, top_refs, m_refs, acc_refs, n_cols, refill=None):
    group = n_cols // len(s_refs)
    pending = None
    ones = jnp.ones((SUM_ROWS, vt.shape[1]), BF16)

    def accumulate(g, alpha, p):
        rows = acc_refs[g].shape[0] - SUM_ROWS
        r0 = 0 if rows == vt.shape[0] else (2 * g * group // n_cols) * rows
        values = jnp.concatenate([vt[r0:r0 + rows, :], ones], axis=0)
        acc_refs[g][...] = alpha * acc_refs[g][...] + _dot(values, p)

    for g in range(len(s_refs)):
        if refill is not None:
            refill(g)
        m_old = m_refs[g][...]
        m_new = jnp.maximum(m_old, top_refs[g][...])
        alpha = jnp.exp2(m_old - m_new)
        p = jnp.exp2(s_refs[g][...] - m_new).astype(BF16)
        m_refs[g][...] = m_new
        if pending is not None:
            accumulate(*pending)
        pending = (g, alpha, p)
    accumulate(*pending)


def _attention_blocks(nq, prepare, keys_of, vt_of, qa_refs, s_slots, top_slots, m_sets, acc_sets,
                      adjust_of, finalize):
    blocks = [(qi, j) for qi in range(nq) for j in range(qi + 1)]
    n_cols = qa_refs[0].shape[1]
    group = n_cols // len(s_slots[0])

    def score(g, qi, j, slot):
        s = _dot(keys_of(j), qa_refs[qi % 2][:, g * group:(g + 1) * group])
        adjust = adjust_of(qi, j)
        if adjust is not None:
            s = adjust(s, g * group)
        s_slots[slot][g][...] = s
        top_slots[slot][g][...] = jnp.max(s, axis=0, keepdims=True)

    prepare(0)
    for g in range(len(s_slots[0])):
        score(g, 0, 0, 0)
    for b, (qi, j) in enumerate(blocks):
        m_refs, acc_refs = m_sets[qi % 2], acc_sets[qi % 2]
        if j == 0:
            for g in range(len(m_refs)):
                _softmax_reset(m_refs[g], acc_refs[g])
        refill = None
        if b + 1 < len(blocks):
            nqi, nj = blocks[b + 1]
            if nqi != qi:
                prepare(nqi)

            def refill(g, nqi=nqi, nj=nj, slot=(b + 1) % 2):
                score(g, nqi, nj, slot)
        _consume_tile(vt_of(j), s_slots[b % 2], top_slots[b % 2], m_refs, acc_refs, n_cols, refill)
        if j == qi:
            finalize(qi, acc_refs)


def _attention_scratch(t, query_rows, value_dim):
    groups = 2 * t // SOFTMAX_GROUP
    shapes = ((t, SOFTMAX_GROUP), (1, SOFTMAX_GROUP), (1, SOFTMAX_GROUP),
              (value_dim + SUM_ROWS, SOFTMAX_GROUP))
    return ([pltpu.VMEM((query_rows, 2 * t), BF16)] * 2
            + [pltpu.VMEM(shape, F32) for shape in shapes for _ in range(2 * groups)])


def _split_scratch(scratch, t):
    G = 2 * t // SOFTMAX_GROUP
    qa_refs, rest = scratch[0:2], scratch[2:]
    kinds = [(rest[k * 2 * G:k * 2 * G + G], rest[k * 2 * G + G:(k + 1) * 2 * G]) for k in range(4)]
    return (qa_refs, *kinds)


def _split_rows(qt_ref, qa_ref, d, t):
    qt = qt_ref[...]
    row = lax.broadcasted_iota(jnp.int32, qt.shape, 0)
    zero = jnp.zeros_like(qt)
    qa_ref[0:2 * d, 0:t] = jnp.where(row < d, qt, zero)
    qa_ref[0:2 * d, t:2 * t] = jnp.where(row >= d, qt, zero)


def _t5_bucket(rel):
    half = REL_BUCKETS // 2
    max_exact = half // 2
    n = jnp.abs(rel)
    ret = jnp.where(rel > 0, half, 0)
    nf = jnp.maximum(n, 1).astype(jnp.float32)
    large = max_exact + (jnp.log(nf / max_exact) / math.log(REL_MAX_DIST / max_exact)
                         * (half - max_exact)).astype(jnp.int32)
    large = jnp.minimum(large, half - 1)
    return ret + jnp.where(n < max_exact, n, large)


def _diff_bias_tiles(rel_bias, t):
    assert t >= REL_MAX_DIST
    kpos = jnp.arange(t)[:, None]
    qpos = jnp.arange(t)[None, :]
    rel = jnp.stack([kpos - qpos - t, kpos - qpos], axis=0)
    bucket = _t5_bucket(rel)
    far = rel_bias[_t5_bucket(jnp.int32(-REL_MAX_DIST))]
    bias = jnp.zeros((rel_bias.shape[1],) + bucket.shape, F32)
    for b in range(REL_BUCKETS):
        bias = jnp.where(bucket == b, (rel_bias[b] - far)[:, None, None, None], bias)
    keep = jnp.stack([jnp.ones((t, t), bool), (kpos // CHUNK) <= (qpos // CHUNK)], axis=0)
    return jnp.where(keep, bias * LOG2E, MASKED)


def _diff_attn_kernel(qt_ref, k_ref, vt_ref, bias_ref, lam_ref, subg_ref, o_ref, *scratch,
                      t, nq, lambda_init):
    W = 2 * DIFF_HEAD_DIM
    qa_refs, s_slots, top_slots, m_sets, acc_sets = _split_scratch(scratch, t)
    lam = lam_ref[...]
    lam_full = (jnp.exp(jnp.sum(lam[0:1] * lam[1:2], axis=-1, keepdims=True))
                - jnp.exp(jnp.sum(lam[2:3] * lam[3:4], axis=-1, keepdims=True)) + lambda_init)

    def prepare(qi):
        _split_rows(qt_ref.at[:, qi * t:(qi + 1) * t], qa_refs[qi % 2], DIFF_HEAD_DIM, t)

    def add_bias(tile):
        def adjust(s, c0):
            return s + bias_ref[tile, :, c0 % t:c0 % t + s.shape[1]]
        return adjust

    def adjust_of(qi, j):
        return add_bias(1) if j == qi else add_bias(0) if j == qi - 1 else None

    def finalize(qi, acc_refs):
        o = jnp.concatenate([acc[0:W, :] / acc[W:W + 1, :] for acc in acc_refs], axis=1)
        o = (o[:, 0:t] - lam_full * o[:, t:2 * t]).T
        o_ref[qi * t:(qi + 1) * t, :] = (_rms(o, subg_ref[...]) * (1.0 - lambda_init)).astype(BF16)

    _attention_blocks(nq, prepare, lambda j: k_ref[j * t:(j + 1) * t, :],
                      lambda j: vt_ref[:, j * t:(j + 1) * t], qa_refs, s_slots, top_slots, m_sets,
                      acc_sets, adjust_of, finalize)


def _diff_attention(qt, k, vt, rel_bias, lam, sub_g, lambda_init):
    B, S, _ = k.shape
    t = min(ATT_TILE, S)
    assert S % t == 0
    H, W = DIFF_HEADS, 2 * DIFF_HEAD_DIM
    bias = _diff_bias_tiles(rel_bias, t)
    return pl.pallas_call(
        functools.partial(_diff_attn_kernel, t=t, nq=S // t, lambda_init=lambda_init),
        out_shape=jax.ShapeDtypeStruct((B, S, H * W), BF16),
        grid=(B, H),
        in_specs=[pl.BlockSpec((None, W, S), lambda b, h: (b, h, 0)),
                  pl.BlockSpec((None, S, W), lambda b, h: (b, 0, h)),
                  pl.BlockSpec((None, W, S), lambda b, h: (b, h, 0)),
                  pl.BlockSpec((None, 2, t, t), lambda b, h: (h, 0, 0, 0)),
                  pl.BlockSpec((4, DIFF_HEAD_DIM), lambda b, h: (0, 0)),
                  pl.BlockSpec((1, W), lambda b, h: (0, 0))],
        out_specs=pl.BlockSpec((None, S, W), lambda b, h: (b, 0, h)),
        scratch_shapes=_attention_scratch(t, W, W),
        compiler_params=_params("parallel", "parallel"), name="diff_attention",
    )(qt, k, vt, bias, lam, sub_g.reshape(1, W))


def _fox_attn_kernel(qt_ref, k_ref, kaug_ref, vt_ref, o_ref, *scratch, t, nq):
    d = FOX_HEAD_DIM
    qa_refs, s_slots, top_slots, m_sets, acc_sets = _split_scratch(scratch, t)
    r = lax.broadcasted_iota(jnp.int32, (LANES, 2 * t), 0)
    c = lax.broadcasted_iota(jnp.int32, (LANES, 2 * t), 1)
    first = DECAY_PIECES * (2 * pl.program_id(1) + jnp.where(c < t, 0, 1))
    pick = jnp.where(r < first, 0.0, jnp.where(r < first + DECAY_PIECES, 1.0, 0.0)).astype(BF16)
    for qa_ref in qa_refs:
        qa_ref[2 * d:, :] = pick

    def prepare(qi):
        _split_rows(qt_ref.at[:, qi * t:(qi + 1) * t], qa_refs[qi % 2], d, t)

    def keys_of(j):
        return jnp.concatenate([k_ref[j * t:(j + 1) * t, :], kaug_ref[j * t:(j + 1) * t, :]], axis=1)

    def causal(s, c0):
        key = lax.broadcasted_iota(jnp.int32, s.shape, 0)
        qry = (lax.broadcasted_iota(jnp.int32, s.shape, 1) + c0) & (t - 1)
        return jnp.where(key <= qry, s, MASKED)

    def finalize(qi, acc_refs):
        o = jnp.concatenate([acc[0:d, :] / acc[d:d + 1, :] for acc in acc_refs], axis=1)
        o = jnp.concatenate([o[:, 0:t], o[:, t:2 * t]], axis=0)
        o_ref[qi * t:(qi + 1) * t, :] = o.T.astype(BF16)

    _attention_blocks(nq, prepare, keys_of, lambda j: vt_ref[:, j * t:(j + 1) * t], qa_refs, s_slots,
                      top_slots, m_sets, acc_sets, lambda qi, j: causal if j == qi else None, finalize)


def _fox_attention(qt, k, kaug, vt):
    B, S, _ = k.shape
    t = min(ATT_TILE, S)
    assert S % t == 0 and (t & (t - 1)) == 0
    H, d = FOX_HEADS, FOX_HEAD_DIM
    W = 2 * d
    return pl.pallas_call(
        functools.partial(_fox_attn_kernel, t=t, nq=S // t),
        out_shape=jax.ShapeDtypeStruct((B, S, H * d), BF16),
        grid=(B, H // 2),
        in_specs=[pl.BlockSpec((None, W, S), lambda b, p: (b, p, 0)),
                  pl.BlockSpec((None, S, W), lambda b, p: (b, 0, p)),
                  pl.BlockSpec((None, S, LANES), lambda b, p: (b, 0, 0)),
                  pl.BlockSpec((None, W, S), lambda b, p: (b, p, 0))],
        out_specs=pl.BlockSpec((None, S, W), lambda b, p: (b, 0, p)),
        scratch_shapes=_attention_scratch(t, W + LANES, d),
        compiler_params=_params("parallel", "parallel"), name="fox_attention",
    )(qt, k, kaug, vt)


def _block_diag_dot(x, w_ref):
    n, bd, tile = w_ref.shape[1], RNN_BLOCK_DIM, MXU_TILE
    outs = []
    for c0 in range(0, n, tile):
        c1 = min(c0 + tile, n)
        k0 = (c0 // bd) * bd // tile * tile
        k1 = min(-(-(-(-c1 // bd) * bd) // tile) * tile, n)
        outs.append(_dot(x[:, k0:k1], w_ref[k0:k1, c0:c1]))
    return jnp.concatenate(outs, axis=1)


def _rglru_kernel(h_ref, gin_ref, win_ref, cw_ref, cb_ref, wr_ref, br_ref, wi_ref, bi_ref, lam_ref,
                  wout_ref, gout_ref, o_ref, ext_ref, xcarry_ref, hcarry_ref, hs_ref, *, tm):
    R = RNN_WIDTH

    @pl.when(pl.program_id(1) == 0)
    def _():
        xcarry_ref[...] = jnp.zeros_like(xcarry_ref)
        hcarry_ref[...] = jnp.zeros_like(hcarry_ref)

    h_in = h_ref[...]
    y = _rms(h_in, gin_ref[...]).astype(BF16)
    x = _dot(y, win_ref[:, R:2 * R])
    ext_ref[0:SUBLANES, :] = xcarry_ref[...]
    ext_ref[SUBLANES:SUBLANES + tm, :] = x
    xcarry_ref[...] = x[tm - SUBLANES:tm, :]
    cw = cw_ref[...]
    xr = x * cw[3:4] + cb_ref[...]
    for k in range(1, RNN_CONV):
        xr = xr + ext_ref[pl.ds(SUBLANES - k, tm), :] * cw[RNN_CONV - 1 - k:RNN_CONV - k]
    xb = xr.astype(BF16)
    r = jax.nn.sigmoid(_block_diag_dot(xb, wr_ref) + br_ref[...])
    i = jax.nn.sigmoid(_block_diag_dot(xb, wi_ref) + bi_ref[...])
    lam = lam_ref[...]
    softplus_neg = jnp.maximum(-lam, 0.0) + jnp.log(1.0 + jnp.exp(-jnp.abs(lam)))
    a = jnp.exp((-RGLRU_C) * r * softplus_neg)
    one_minus_a2 = 1.0 - a * a
    u = one_minus_a2 * lax.rsqrt(jnp.maximum(one_minus_a2, TINY)) * (i * xr)

    row = lax.broadcasted_iota(jnp.int32, (tm, R), 0) % SUBLANES
    shift = 1
    while shift < SUBLANES:
        live = row >= shift
        a_prev = jnp.where(live, pltpu.roll(a, shift, axis=0), 1.0)
        u_prev = jnp.where(live, pltpu.roll(u, shift, axis=0), 0.0)
        u = a * u_prev + u
        a = a * a_prev
        shift *= 2
    carry = hcarry_ref[...]
    for g in range(tm // SUBLANES):
        rows = slice(g * SUBLANES, (g + 1) * SUBLANES)
        hg = a[rows] * carry + u[rows]
        hs_ref[rows, :] = hg
        carry = jnp.broadcast_to(hg[SUBLANES - 1:SUBLANES, :], (SUBLANES, R))
    hcarry_ref[...] = carry
    hs = hs_ref[...]

    gate = jax.nn.gelu(_dot(y, win_ref[:, 0:R]))
    out = _dot((hs * gate).astype(BF16), wout_ref[...])
    o_ref[...] = h_in + _rms(out, gout_ref[...])


def _block_diag(w):
    n, c, d = w.shape
    eye = jnp.eye(n, dtype=w.dtype)
    return (w[:, :, None, :] * eye[:, None, :, None]).reshape(n * c, n * d)


def _rglru_layer(h, g_in, w_in, conv_w, conv_b, w_r, b_r, w_i, b_i, lam, w_out, g_out):
    B, S, D = h.shape
    tm = min(RNN_TILE, S)
    assert S % tm == 0
    R = RNN_WIDTH
    vec = lambda a: a.reshape(1, -1)
    return pl.pallas_call(
        functools.partial(_rglru_kernel, tm=tm),
        out_shape=jax.ShapeDtypeStruct((B, S, D), F32),
        grid=(B, S // tm),
        in_specs=[_rows(tm, D), _resident((1, D)), _resident((D, 2 * R)), _resident((RNN_CONV, R)),
                  _resident((1, R)), _resident((R, R)), _resident((1, R)), _resident((R, R)),
                  _resident((1, R)), _resident((1, R)), _resident((R, D)), _resident((1, D))],
        out_specs=_rows(tm, D),
        scratch_shapes=[pltpu.VMEM((SUBLANES + tm, R), F32), pltpu.VMEM((SUBLANES, R), F32),
                        pltpu.VMEM((SUBLANES, R), F32), pltpu.VMEM((tm, R), F32)],
        compiler_params=_params("parallel", "arbitrary"), name="rglru_mixer",
    )(h, vec(g_in), w_in.astype(BF16), conv_w, vec(conv_b), _block_diag(w_r).astype(BF16), vec(b_r),
      _block_diag(w_i).astype(BF16), vec(b_i), vec(lam), w_out.astype(BF16), vec(g_out))


def kernel(x, norm_g, ffn_w_up, ffn_conv_w, ffn_conv_b, ffn_w_down, rel_bias, a_w_in, a_ln_g, a_ln_b, a_w_s, a_b_s, a_w_out, b_w_in, b_lam, b_sub_g, b_w_out, c_w_in, c_b_f, c_w_out, d_w_in, d_conv_w, d_conv_b, d_w_r, d_b_r, d_w_i, d_b_i, d_lam, d_w_out):
    depth = norm_g.shape[0]
    h = x
    for layer in range(depth):
        m = layer % N_MIXERS
        j = layer // N_MIXERS
        g = norm_g[layer]
        if m == 0:
            h = _gmlp_layer(h, g[0], a_w_in[j], a_ln_g[j], a_ln_b[j], a_w_s[j], a_b_s[j], a_w_out[j], g[1])
        elif m == 1:
            lambda_init = 0.8 - 0.6 * math.exp(-0.3 * layer)
            n = DIFF_HEADS * 2 * DIFF_HEAD_DIM
            w = b_w_in[j]
            qt, k, vt = _attn_projections(h, g[0], w[:, :n], w[:, n:2 * n], w[:, 2 * n:],
                                          DIFF_HEAD_DIM ** -0.5 * LOG2E)
            o = _diff_attention(qt, k, vt, rel_bias, b_lam[j], b_sub_g[j], lambda_init)
            h = _out_proj(o, h, b_w_out[j], g[1])
        elif m == 2:
            n = FOX_HEADS * FOX_HEAD_DIM
            w = c_w_in[j]
            qt, k, vt, kaug = _attn_projections(h, g[0], w[:, :n], w[:, n:2 * n], w[:, 2 * n:3 * n],
                                                FOX_HEAD_DIM ** -0.5 * LOG2E, fox=(w[:, 3 * n:], c_b_f[j]))
            o = _fox_attention(qt, k, kaug, vt)
            h = _out_proj(o, h, c_w_out[j], g[1])
        else:
            h = _rglru_layer(h, g[0], d_w_in[j], d_conv_w[j], d_conv_b[j], d_w_r[j], d_b_r[j],
                             d_w_i[j], d_b_i[j], d_lam[j], d_w_out[j], g[1])
        h = _conv_ffn(h, g[2], ffn_w_up[layer], ffn_conv_w[layer], ffn_conv_b[layer],
                      ffn_w_down[layer], g[3])
    return h
```

```python
import functools
import math

import numpy as np
import jax
import jax.numpy as jnp
from jax import lax
from jax.experimental import pallas as pl
from jax.experimental.pallas import tpu as pltpu

F32 = jnp.float32
BF16 = jnp.bfloat16

D_MODEL = 1024
CHUNK = 64
N_MIXERS = 4
EPS = 1e-6
GMLP_CHUNK = 128
GMLP_WIDTH = 1024
GMLP_GROUPS = 8
DIFF_HEADS = 8
DIFF_HEAD_DIM = 64
FOX_HEADS = 16
FOX_HEAD_DIM = 64
RNN_WIDTH = 1280
RNN_BLOCKS = 16
RNN_BLOCK_DIM = RNN_WIDTH // RNN_BLOCKS
RNN_CONV = 4
RGLRU_C = 8.0
D_FF = 2816
FFN_CONV = 3
REL_BUCKETS = 32
REL_MAX_DIST = 128

LANES = 128
SUBLANES = 8
MXU_TILE = 256
MASKED = -1e30
TINY = 1e-30
VMEM_LIMIT = 56 * 1024 * 1024

FFN_TILE = 512
FFN_CHUNK = 256
FFN_SLOTS = 6
ROW_TILE = 512
GMLP_TILE = 512
RNN_TILE = 256
ATT_TILE = 512
SOFTMAX_GROUP = 256
SUM_ROWS = 16
DECAY_PIECES = 3
CUMSUM_BLOCK = 128
LOG2E = math.log2(math.e)

_NT = (((1,), (1,)), ((), ()))


def _params(*sem):
    return pltpu.CompilerParams(dimension_semantics=sem, vmem_limit_bytes=VMEM_LIMIT)


def _resident(shape):
    zeros = (0,) * len(shape)
    return pl.BlockSpec(shape, lambda *_: zeros, pipeline_mode=pl.Buffered(1))


def _rows(tm, width):
    return pl.BlockSpec((None, tm, width), lambda b, s: (b, s, 0))


def _cols(height, tm):
    return pl.BlockSpec((None, height, tm), lambda b, s: (b, 0, s))


def _rms(x, g):
    return x * lax.rsqrt(jnp.mean(x * x, axis=-1, keepdims=True) + EPS) * g


def _dot(a, b):
    return jnp.dot(a, b, preferred_element_type=F32)


def _ffn_kernel(*refs, tm, nc, fc, mixer_out):
    if mixer_out:
        mix_ref, wmix_ref, gmix_ref = refs[:3]
        refs = refs[3:]
    (h_ref, gin_ref, wup_ref, cw_ref, cb_ref, wdn_ref, gout_ref, o_ref,
     carry_ref, acc_ref, y_ref, *bufs) = refs
    ext_refs = tuple((bufs[2 * i], bufs[2 * i + 1]) for i in range(FFN_SLOTS))
    act_refs = bufs[2 * FFN_SLOTS:]

    @pl.when(pl.program_id(1) == 0)
    def _():
        carry_ref[...] = jnp.zeros_like(carry_ref)

    h = h_ref[...]
    if mixer_out:
        h = h + _rms(_dot(mix_ref[...], wmix_ref[...]), gmix_ref[...])
    y_ref[...] = _rms(h, gin_ref[...]).astype(BF16)

    def up(c, slot):
        for branch in range(2):
            idx = c + branch * nc
            ext = ext_refs[slot][branch]
            u = _dot(y_ref[...], wup_ref[:, idx * fc:(idx + 1) * fc])
            ext[0:SUBLANES, :] = carry_ref[idx]
            ext[SUBLANES:SUBLANES + tm, :] = u
            carry_ref[idx] = u[tm - SUBLANES:tm, :]

    def activate(c, slot):
        conv = []
        for branch in range(2):
            idx = c + branch * nc
            ext = ext_refs[slot][branch]
            w = cw_ref[:, idx * fc:(idx + 1) * fc]
            conv.append(ext[pl.ds(SUBLANES, tm), :] * w[2:3] + ext[pl.ds(SUBLANES - 1, tm), :] * w[1:2]
                        + ext[pl.ds(SUBLANES - 2, tm), :] * w[0:1] + cb_ref[:, idx * fc:(idx + 1) * fc])
        act_refs[slot][...] = (jax.nn.gelu(conv[0]) * conv[1]).astype(BF16)

    def down(c, slot):
        part = _dot(act_refs[slot][...], wdn_ref[c * fc:(c + 1) * fc, :])
        if c == 0:
            acc_ref[...] = part
        else:
            acc_ref[...] += part

    lag = FFN_SLOTS - 1
    for c in range(min(lag, nc)):
        up(c, c % FFN_SLOTS)
    for c in range(nc + lag):
        if lag <= c:
            down(c - lag, (c - lag) % FFN_SLOTS)
        if c + lag < nc:
            up(c + lag, (c + lag) % FFN_SLOTS)
        if c < nc:
            activate(c, c % FFN_SLOTS)
    o_ref[...] = h + _rms(acc_ref[...], gout_ref[...])


def _conv_ffn(h, g_in, w_up, conv_w, conv_b, w_down, g_out, mixer_out=None):
    B, S, D = h.shape
    tm, fc = min(FFN_TILE, S), FFN_CHUNK
    nc = D_FF // fc
    assert D_FF % fc == 0 and S % tm == 0
    wup, cw, cb, wdn = w_up.astype(BF16), conv_w, conv_b.reshape(1, -1), w_down.astype(BF16)
    mix_specs, mix_args = [], []
    if mixer_out is not None:
        mix, w_mix, g_mix = mixer_out
        n = mix.shape[-1]
        mix_specs = [_rows(tm, n), _resident((n, D)), _resident((1, D))]
        mix_args = [mix, w_mix.astype(BF16), g_mix.reshape(1, D)]
    return pl.pallas_call(
        functools.partial(_ffn_kernel, tm=tm, nc=nc, fc=fc, mixer_out=mixer_out is not None),
        out_shape=jax.ShapeDtypeStruct((B, S, D), F32),
        grid=(B, S // tm),
        in_specs=mix_specs + [_rows(tm, D), _resident((1, D)), _resident(wup.shape), _resident(cw.shape),
                              _resident(cb.shape), _resident(wdn.shape), _resident((1, D))],
        out_specs=_rows(tm, D),
        scratch_shapes=[pltpu.VMEM((2 * nc, SUBLANES, fc), F32),
                        pltpu.VMEM((tm, D), F32),
                        pltpu.VMEM((tm, D), BF16)]
        + [pltpu.VMEM((SUBLANES + tm, fc), F32)] * (2 * FFN_SLOTS)
        + [pltpu.VMEM((tm, fc), BF16)] * FFN_SLOTS,
        compiler_params=_params("parallel", "arbitrary"),
        name="conv_ffn",
    )(*mix_args, h, g_in.reshape(1, D), wup, cw, cb, wdn, g_out.reshape(1, D))


def _gmlp_kernel(h_ref, gin_ref, win_ref, lng_ref, lnb_ref, ws_ref, bs_ref, wout_ref, gout_ref,
                 o_ref, vo_ref, *, tm):
    W, G, C = GMLP_WIDTH, GMLP_GROUPS, GMLP_CHUNK
    h = h_ref[...]
    y = _rms(h, gin_ref[...]).astype(BF16)
    v_pre = _dot(y, win_ref[:, W:2 * W])
    u_pre = _dot(y, win_ref[:, 0:W])
    v = jax.nn.gelu(v_pre)
    mu = jnp.mean(v, axis=-1, keepdims=True)
    vc = v - mu
    var = jnp.mean(vc * vc, axis=-1, keepdims=True)
    vb = (vc * lax.rsqrt(var + EPS) * lng_ref[...] + lnb_ref[...]).astype(BF16)
    p_out = lax.broadcasted_iota(jnp.int32, (C, C), 0)
    q_in = lax.broadcasted_iota(jnp.int32, (C, C), 1)
    keep = (q_in // CHUNK) <= (p_out // CHUNK)
    for g in range(G):
        wsg = jnp.where(keep, ws_ref[g], 0.0).astype(BF16)
        for n in range(tm // C):
            blk = _dot(wsg, vb[n * C:(n + 1) * C, g * C:(g + 1) * C])
            vo_ref[n * C:(n + 1) * C, g * C:(g + 1) * C] = blk + bs_ref[:, g * C:(g + 1) * C]
    out = _dot((jax.nn.gelu(u_pre) * vo_ref[...]).astype(BF16), wout_ref[...])
    o_ref[...] = h + _rms(out, gout_ref[...])


def _gmlp_layer(h, g_in, w_in, ln_g, ln_b, w_s, b_s, w_out, g_out):
    B, S, D = h.shape
    tm = min(GMLP_TILE, S)
    assert S % tm == 0 and tm % GMLP_CHUNK == 0
    W = GMLP_WIDTH
    bs = jnp.repeat(b_s.T, GMLP_CHUNK, axis=1)
    return pl.pallas_call(
        functools.partial(_gmlp_kernel, tm=tm),
        out_shape=jax.ShapeDtypeStruct((B, S, D), F32),
        grid=(B, S // tm),
        in_specs=[_rows(tm, D), _resident((1, D)), _resident((D, 2 * W)), _resident((1, W)),
                  _resident((1, W)), _resident(w_s.shape), _resident(bs.shape),
                  _resident((W, D)), _resident((1, D))],
        out_specs=_rows(tm, D),
        scratch_shapes=[pltpu.VMEM((tm, W), F32)],
        compiler_params=_params("parallel", "parallel"),
        name="gmlp_mixer",
    )(h, g_in.reshape(1, D), w_in.astype(BF16), ln_g.reshape(1, W), ln_b.reshape(1, W), w_s, bs,
      w_out.astype(BF16), g_out.reshape(1, D))


def _project_qkv(h_ref, gin_ref, wqt_ref, wk_ref, wvt_ref, qt_ref, k_ref, vt_ref, scale):
    y = _rms(h_ref[...], gin_ref[...]).astype(BF16)
    qt = lax.dot_general(wqt_ref[...], y, _NT, preferred_element_type=F32)
    qt_ref[...] = (qt * scale).astype(BF16)
    k_ref[...] = _dot(y, wk_ref[...]).astype(BF16)
    vt_ref[...] = lax.dot_general(wvt_ref[...], y, _NT, preferred_element_type=F32).astype(BF16)
    return y


def _qkv_kernel(h_ref, gin_ref, wqt_ref, wk_ref, wvt_ref, qt_ref, k_ref, vt_ref, *, scale):
    _project_qkv(h_ref, gin_ref, wqt_ref, wk_ref, wvt_ref, qt_ref, k_ref, vt_ref, scale)


def _fox_proj_kernel(h_ref, gin_ref, wqt_ref, wk_ref, wvt_ref, wf_ref, bf_ref, place_ref,
                     qt_ref, k_ref, vt_ref, kaug_ref, carry_ref, *, scale, tm):
    @pl.when(pl.program_id(1) == 0)
    def _():
        carry_ref[...] = jnp.zeros_like(carry_ref)

    y = _project_qkv(h_ref, gin_ref, wqt_ref, wk_ref, wvt_ref, qt_ref, k_ref, vt_ref, scale)
    f = _dot(y, wf_ref[...]) + bf_ref[...]
    logf = jnp.minimum(f, 0.0) - jnp.log(1.0 + jnp.exp(-jnp.abs(f)))
    n = CUMSUM_BLOCK
    row = lax.broadcasted_iota(jnp.int32, (n, n), 0)
    col = lax.broadcasted_iota(jnp.int32, (n, n), 1)
    tri = jnp.where(col <= row, 1.0, 0.0).astype(BF16)
    total = carry_ref[0:1, :]
    for r0 in range(0, tm, n):
        block = _dot_pieces(tri, logf[r0:r0 + n, :]) + total
        kaug_ref[r0:r0 + n, :] = _dot_pieces(block * (-LOG2E), place_ref, left=True).astype(BF16)
        total = block[n - 1:n, :]
    carry_ref[...] = jnp.broadcast_to(total, carry_ref.shape)


def _dot_pieces(a, b, left=False):
    rest = a if left else b
    out = None
    for c in range(DECAY_PIECES):
        piece = rest.astype(BF16)
        rest = rest - piece.astype(F32)
        term = _dot(piece, b[c]) if left else _dot(a, piece)
        out = term if out is None else out + term
    return out


def _decay_placement(n_heads):
    assert DECAY_PIECES * n_heads <= LANES
    place = np.zeros((DECAY_PIECES, LANES, LANES), np.float32)
    for h in range(n_heads):
        for c in range(DECAY_PIECES):
            place[c, h, DECAY_PIECES * h + c] = 1.0
    return jnp.asarray(place, BF16)


def _attn_projections(h, g_in, wq, wk, wv, scale, fox=None):
    B, S, D = h.shape
    tm = min(ROW_TILE, S)
    assert S % tm == 0
    nq, nk, nv = wq.shape[1], wk.shape[1], wv.shape[1]
    in_specs = [_rows(tm, D), _resident((1, D)), _resident((nq, D)), _resident((D, nk)),
                _resident((nv, D))]
    out_shape = [jax.ShapeDtypeStruct((B, nq, S), BF16), jax.ShapeDtypeStruct((B, S, nk), BF16),
                 jax.ShapeDtypeStruct((B, nv, S), BF16)]
    out_specs = [_cols(nq, tm), _rows(tm, nk), _cols(nv, tm)]
    args = [h, g_in.reshape(1, D), wq.T.astype(BF16), wk.astype(BF16), wv.T.astype(BF16)]
    if fox is None:
        return pl.pallas_call(
            functools.partial(_qkv_kernel, scale=scale),
            out_shape=out_shape, grid=(B, S // tm), in_specs=in_specs, out_specs=out_specs,
            compiler_params=_params("parallel", "parallel"), name="qkv_proj",
        )(*args)
    wf, bf = fox
    H = wf.shape[1]
    wf_pad = jnp.zeros((D, LANES), F32).at[:, :H].set(wf).astype(BF16)
    bf_pad = jnp.zeros((1, LANES), F32).at[0, :H].set(bf)
    place = _decay_placement(H)
    in_specs += [_resident((D, LANES)), _resident((1, LANES)), _resident(place.shape)]
    out_shape += [jax.ShapeDtypeStruct((B, S, place.shape[2]), BF16)]
    out_specs += [_rows(tm, place.shape[2])]
    return pl.pallas_call(
        functools.partial(_fox_proj_kernel, scale=scale, tm=tm),
        out_shape=out_shape, grid=(B, S // tm), in_specs=in_specs, out_specs=out_specs,
        scratch_shapes=[pltpu.VMEM((SUBLANES, LANES), F32)],
        compiler_params=_params("parallel", "arbitrary"), name="fox_proj",
    )(*args, wf_pad, bf_pad, place)


def _softmax_reset(m_ref, acc_ref):
    m_ref[...] = jnp.full(m_ref.shape, MASKED, F32)
    acc_ref[...] = jnp.zeros_like(acc_ref)


def _consume_tile(vt, s_refs, top_refs, m_refs, acc_refs, n_cols, refill=None):
    group = n_cols // len(s_refs)
    pending = None
    ones = jnp.ones((SUM_ROWS, vt.shape[1]), BF16)

    def accumulate(g, alpha, p):
        rows = acc_refs[g].shape[0] - SUM_ROWS
        r0 = 0 if rows == vt.shape[0] else (2 * g * group // n_cols) * rows
        values = jnp.concatenate([vt[r0:r0 + rows, :], ones], axis=0)
        acc_refs[g][...] = alpha * acc_refs[g][...] + _dot(values, p)

    for g in range(len(s_refs)):
        if refill is not None:
            refill(g)
        m_old = m_refs[g][...]
        m_new = jnp.maximum(m_old, top_refs[g][...])
        alpha = jnp.exp2(m_old - m_new)
        p = jnp.exp2(s_refs[g][...] - m_new).astype(BF16)
        m_refs[g][...] = m_new
        if pending is not None:
            accumulate(*pending)
        pending = (g, alpha, p)
    accumulate(*pending)


def _attention_blocks(nq, prepare, keys_of, vt_of, qa_refs, s_slots, top_slots, m_sets, acc_sets,
                      adjust_of, finalize):
    blocks = [(qi, j) for qi in range(nq) for j in range(qi + 1)]
    n_cols = qa_refs[0].shape[1]
    group = n_cols // len(s_slots[0])

    def score(g, qi, j, slot):
        s = _dot(keys_of(j), qa_refs[qi % 2][:, g * group:(g + 1) * group])
        adjust = adjust_of(qi, j)
        if adjust is not None:
            s = adjust(s, g * group)
        s_slots[slot][g][...] = s
        top_slots[slot][g][...] = jnp.max(s, axis=0, keepdims=True)

    prepare(0)
    for g in range(len(s_slots[0])):
        score(g, 0, 0, 0)
    for b, (qi, j) in enumerate(blocks):
        m_refs, acc_refs = m_sets[qi % 2], acc_sets[qi % 2]
        if j == 0:
            for g in range(len(m_refs)):
                _softmax_reset(m_refs[g], acc_refs[g])
        refill = None
        if b + 1 < len(blocks):
            nqi, nj = blocks[b + 1]
            if nqi != qi:
                prepare(nqi)

            def refill(g, nqi=nqi, nj=nj, slot=(b + 1) % 2):
                score(g, nqi, nj, slot)
        _consume_tile(vt_of(j), s_slots[b % 2], top_slots[b % 2], m_refs, acc_refs, n_cols, refill)
        if j == qi:
            finalize(qi, acc_refs)


def _attention_scratch(t, query_rows, value_dim):
    groups = 2 * t // SOFTMAX_GROUP
    shapes = ((t, SOFTMAX_GROUP), (1, SOFTMAX_GROUP), (1, SOFTMAX_GROUP),
              (value_dim + SUM_ROWS, SOFTMAX_GROUP))
    return ([pltpu.VMEM((query_rows, 2 * t), BF16)] * 2
            + [pltpu.VMEM(shape, F32) for shape in shapes for _ in range(2 * groups)])


def _split_scratch(scratch, t):
    G = 2 * t // SOFTMAX_GROUP
    qa_refs, rest = scratch[0:2], scratch[2:]
    kinds = [(rest[k * 2 * G:k * 2 * G + G], rest[k * 2 * G + G:(k + 1) * 2 * G]) for k in range(4)]
    return (qa_refs, *kinds)


def _split_rows(qt_ref, qa_ref, d, t):
    qt = qt_ref[...]
    row = lax.broadcasted_iota(jnp.int32, qt.shape, 0)
    zero = jnp.zeros_like(qt)
    qa_ref[0:2 * d, 0:t] = jnp.where(row < d, qt, zero)
    qa_ref[0:2 * d, t:2 * t] = jnp.where(row >= d, qt, zero)


def _t5_bucket(rel):
    half = REL_BUCKETS // 2
    max_exact = half // 2
    n = jnp.abs(rel)
    ret = jnp.where(rel > 0, half, 0)
    nf = jnp.maximum(n, 1).astype(jnp.float32)
    large = max_exact + (jnp.log(nf / max_exact) / math.log(REL_MAX_DIST / max_exact)
                         * (half - max_exact)).astype(jnp.int32)
    large = jnp.minimum(large, half - 1)
    return ret + jnp.where(n < max_exact, n, large)


def _diff_bias_tiles(rel_bias, t):
    assert t >= REL_MAX_DIST
    kpos = jnp.arange(t)[:, None]
    qpos = jnp.arange(t)[None, :]
    rel = jnp.stack([kpos - qpos - t, kpos - qpos], axis=0)
    bucket = _t5_bucket(rel)
    far = rel_bias[_t5_bucket(jnp.int32(-REL_MAX_DIST))]
    bias = jnp.zeros((rel_bias.shape[1],) + bucket.shape, F32)
    for b in range(REL_BUCKETS):
        bias = jnp.where(bucket == b, (rel_bias[b] - far)[:, None, None, None], bias)
    keep = jnp.stack([jnp.ones((t, t), bool), (kpos // CHUNK) <= (qpos // CHUNK)], axis=0)
    return jnp.where(keep, bias * LOG2E, MASKED)


def _diff_attn_kernel(qt_ref, k_ref, vt_ref, bias_ref, lam_ref, subg_ref, o_ref, *scratch,
                      t, nq, lambda_init):
    W = 2 * DIFF_HEAD_DIM
    qa_refs, s_slots, top_slots, m_sets, acc_sets = _split_scratch(scratch, t)
    lam = lam_ref[...]
    lam_full = (jnp.exp(jnp.sum(lam[0:1] * lam[1:2], axis=-1, keepdims=True))
                - jnp.exp(jnp.sum(lam[2:3] * lam[3:4], axis=-1, keepdims=True)) + lambda_init)

    def prepare(qi):
        _split_rows(qt_ref.at[:, qi * t:(qi + 1) * t], qa_refs[qi % 2], DIFF_HEAD_DIM, t)

    def add_bias(tile):
        def adjust(s, c0):
            return s + bias_ref[tile, :, c0 % t:c0 % t + s.shape[1]]
        return adjust

    def adjust_of(qi, j):
        return add_bias(1) if j == qi else add_bias(0) if j == qi - 1 else None

    def finalize(qi, acc_refs):
        o = jnp.concatenate([acc[0:W, :] / acc[W:W + 1, :] for acc in acc_refs], axis=1)
        o = (o[:, 0:t] - lam_full * o[:, t:2 * t]).T
        o_ref[qi * t:(qi + 1) * t, :] = (_rms(o, subg_ref[...]) * (1.0 - lambda_init)).astype(BF16)

    _attention_blocks(nq, prepare, lambda j: k_ref[j * t:(j + 1) * t, :],
                      lambda j: vt_ref[:, j * t:(j + 1) * t], qa_refs, s_slots, top_slots, m_sets,
                      acc_sets, adjust_of, finalize)


def _diff_attention(qt, k, vt, rel_bias, lam, sub_g, lambda_init):
    B, S, _ = k.shape
    t = min(ATT_TILE, S)
    assert S % t == 0
    H, W = DIFF_HEADS, 2 * DIFF_HEAD_DIM
    bias = _diff_bias_tiles(rel_bias, t)
    return pl.pallas_call(
        functools.partial(_diff_attn_kernel, t=t, nq=S // t, lambda_init=lambda_init),
        out_shape=jax.ShapeDtypeStruct((B, S, H * W), BF16),
        grid=(B, H),
        in_specs=[pl.BlockSpec((None, W, S), lambda b, h: (b, h, 0)),
                  pl.BlockSpec((None, S, W), lambda b, h: (b, 0, h)),
                  pl.BlockSpec((None, W, S), lambda b, h: (b, h, 0)),
                  pl.BlockSpec((None, 2, t, t), lambda b, h: (h, 0, 0, 0)),
                  pl.BlockSpec((4, DIFF_HEAD_DIM), lambda b, h: (0, 0)),
                  pl.BlockSpec((1, W), lambda b, h: (0, 0))],
        out_specs=pl.BlockSpec((None, S, W), lambda b, h: (b, 0, h)),
        scratch_shapes=_attention_scratch(t, W, W),
        compiler_params=_params("parallel", "parallel"), name="diff_attention",
    )(qt, k, vt, bias, lam, sub_g.reshape(1, W))


def _fox_attn_kernel(qt_ref, k_ref, kaug_ref, vt_ref, o_ref, *scratch, t, nq):
    d = FOX_HEAD_DIM
    qa_refs, s_slots, top_slots, m_sets, acc_sets = _split_scratch(scratch, t)
    r = lax.broadcasted_iota(jnp.int32, (LANES, 2 * t), 0)
    c = lax.broadcasted_iota(jnp.int32, (LANES, 2 * t), 1)
    first = DECAY_PIECES * (2 * pl.program_id(1) + jnp.where(c < t, 0, 1))
    pick = jnp.where(r < first, 0.0, jnp.where(r < first + DECAY_PIECES, 1.0, 0.0)).astype(BF16)
    for qa_ref in qa_refs:
        qa_ref[2 * d:, :] = pick

    def prepare(qi):
        _split_rows(qt_ref.at[:, qi * t:(qi + 1) * t], qa_refs[qi % 2], d, t)

    def keys_of(j):
        return jnp.concatenate([k_ref[j * t:(j + 1) * t, :], kaug_ref[j * t:(j + 1) * t, :]], axis=1)

    def causal(s, c0):
        key = lax.broadcasted_iota(jnp.int32, s.shape, 0)
        qry = (lax.broadcasted_iota(jnp.int32, s.shape, 1) + c0) & (t - 1)
        return jnp.where(key <= qry, s, MASKED)

    def finalize(qi, acc_refs):
        o = jnp.concatenate([acc[0:d, :] / acc[d:d + 1, :] for acc in acc_refs], axis=1)
        o = jnp.concatenate([o[:, 0:t], o[:, t:2 * t]], axis=0)
        o_ref[qi * t:(qi + 1) * t, :] = o.T.astype(BF16)

    _attention_blocks(nq, prepare, keys_of, lambda j: vt_ref[:, j * t:(j + 1) * t], qa_refs, s_slots,
                      top_slots, m_sets, acc_sets, lambda qi, j: causal if j == qi else None, finalize)


def _fox_attention(qt, k, kaug, vt):
    B, S, _ = k.shape
    t = min(ATT_TILE, S)
    assert S % t == 0 and (t & (t - 1)) == 0
    H, d = FOX_HEADS, FOX_HEAD_DIM
    W = 2 * d
    return pl.pallas_call(
        functools.partial(_fox_attn_kernel, t=t, nq=S // t),
        out_shape=jax.ShapeDtypeStruct((B, S, H * d), BF16),
        grid=(B, H // 2),
        in_specs=[pl.BlockSpec((None, W, S), lambda b, p: (b, p, 0)),
                  pl.BlockSpec((None, S, W), lambda b, p: (b, 0, p)),
                  pl.BlockSpec((None, S, LANES), lambda b, p: (b, 0, 0)),
                  pl.BlockSpec((None, W, S), lambda b, p: (b, p, 0))],
        out_specs=pl.BlockSpec((None, S, W), lambda b, p: (b, 0, p)),
        scratch_shapes=_attention_scratch(t, W + LANES, d),
        compiler_params=_params("parallel", "parallel"), name="fox_attention",
    )(qt, k, kaug, vt)


def _block_diag_dot(x, w_ref):
    n, bd, tile = w_ref.shape[1], RNN_BLOCK_DIM, MXU_TILE
    outs = []
    for c0 in range(0, n, tile):
        c1 = min(c0 + tile, n)
        k0 = (c0 // bd) * bd // tile * tile
        k1 = min(-(-(-(-c1 // bd) * bd) // tile) * tile, n)
        outs.append(_dot(x[:, k0:k1], w_ref[k0:k1, c0:c1]))
    return jnp.concatenate(outs, axis=1)


def _rglru_kernel(h_ref, gin_ref, win_ref, cw_ref, cb_ref, wr_ref, br_ref, wi_ref, bi_ref, lam_ref,
                  wout_ref, gout_ref, o_ref, ext_ref, xcarry_ref, hcarry_ref, hs_ref, *, tm):
    R = RNN_WIDTH

    @pl.when(pl.program_id(1) == 0)
    def _():
        xcarry_ref[...] = jnp.zeros_like(xcarry_ref)
        hcarry_ref[...] = jnp.zeros_like(hcarry_ref)

    h_in = h_ref[...]
    y = _rms(h_in, gin_ref[...]).astype(BF16)
    x = _dot(y, win_ref[:, R:2 * R])
    ext_ref[0:SUBLANES, :] = xcarry_ref[...]
    ext_ref[SUBLANES:SUBLANES + tm, :] = x
    xcarry_ref[...] = x[tm - SUBLANES:tm, :]
    cw = cw_ref[...]
    xr = x * cw[3:4] + cb_ref[...]
    for k in range(1, RNN_CONV):
        xr = xr + ext_ref[pl.ds(SUBLANES - k, tm), :] * cw[RNN_CONV - 1 - k:RNN_CONV - k]
    xb = xr.astype(BF16)
    r = jax.nn.sigmoid(_block_diag_dot(xb, wr_ref) + br_ref[...])
    i = jax.nn.sigmoid(_block_diag_dot(xb, wi_ref) + bi_ref[...])
    lam = lam_ref[...]
    softplus_neg = jnp.maximum(-lam, 0.0) + jnp.log(1.0 + jnp.exp(-jnp.abs(lam)))
    a = jnp.exp((-RGLRU_C) * r * softplus_neg)
    one_minus_a2 = 1.0 - a * a
    u = one_minus_a2 * lax.rsqrt(jnp.maximum(one_minus_a2, TINY)) * (i * xr)

    row = lax.broadcasted_iota(jnp.int32, (tm, R), 0) % SUBLANES
    shift = 1
    while shift < SUBLANES:
        live = row >= shift
        a_prev = jnp.where(live, pltpu.roll(a, shift, axis=0), 1.0)
        u_prev = jnp.where(live, pltpu.roll(u, shift, axis=0), 0.0)
        u = a * u_prev + u
        a = a * a_prev
        shift *= 2
    carry = hcarry_ref[...]
    for g in range(tm // SUBLANES):
        rows = slice(g * SUBLANES, (g + 1) * SUBLANES)
        hg = a[rows] * carry + u[rows]
        hs_ref[rows, :] = hg
        carry = jnp.broadcast_to(hg[SUBLANES - 1:SUBLANES, :], (SUBLANES, R))
    hcarry_ref[...] = carry
    hs = hs_ref[...]

    gate = jax.nn.gelu(_dot(y, win_ref[:, 0:R]))
    out = _dot((hs * gate).astype(BF16), wout_ref[...])
    o_ref[...] = h_in + _rms(out, gout_ref[...])


def _block_diag(w):
    n, c, d = w.shape
    eye = jnp.eye(n, dtype=w.dtype)
    return (w[:, :, None, :] * eye[:, None, :, None]).reshape(n * c, n * d)


def _rglru_layer(h, g_in, w_in, conv_w, conv_b, w_r, b_r, w_i, b_i, lam, w_out, g_out):
    B, S, D = h.shape
    tm = min(RNN_TILE, S)
    assert S % tm == 0
    R = RNN_WIDTH
    vec = lambda a: a.reshape(1, -1)
    return pl.pallas_call(
        functools.partial(_rglru_kernel, tm=tm),
        out_shape=jax.ShapeDtypeStruct((B, S, D), F32),
        grid=(B, S // tm),
        in_specs=[_rows(tm, D), _resident((1, D)), _resident((D, 2 * R)), _resident((RNN_CONV, R)),
                  _resident((1, R)), _resident((R, R)), _resident((1, R)), _resident((R, R)),
                  _resident((1, R)), _resident((1, R)), _resident((R, D)), _resident((1, D))],
        out_specs=_rows(tm, D),
        scratch_shapes=[pltpu.VMEM((SUBLANES + tm, R), F32), pltpu.VMEM((SUBLANES, R), F32),
                        pltpu.VMEM((SUBLANES, R), F32), pltpu.VMEM((tm, R), F32)],
        compiler_params=_params("parallel", "arbitrary"), name="rglru_mixer",
    )(h, vec(g_in), w_in.astype(BF16), conv_w, vec(conv_b), _block_diag(w_r).astype(BF16), vec(b_r),
      _block_diag(w_i).astype(BF16), vec(b_i), vec(lam), w_out.astype(BF16), vec(g_out))


def kernel(x, norm_g, ffn_w_up, ffn_conv_w, ffn_conv_b, ffn_w_down, rel_bias, a_w_in, a_ln_g, a_ln_b, a_w_s, a_b_s, a_w_out, b_w_in, b_lam, b_sub_g, b_w_out, c_w_in, c_b_f, c_w_out, d_w_in, d_conv_w, d_conv_b, d_w_r, d_b_r, d_w_i, d_b_i, d_lam, d_w_out):
    depth = norm_g.shape[0]
    h = x
    for layer in range(depth):
        m = layer % N_MIXERS
        j = layer // N_MIXERS
        g = norm_g[layer]
        mixer_out = None
        if m == 0:
            h = _gmlp_layer(h, g[0], a_w_in[j], a_ln_g[j], a_ln_b[j], a_w_s[j], a_b_s[j], a_w_out[j], g[1])
        elif m == 1:
            lambda_init = 0.8 - 0.6 * math.exp(-0.3 * layer)
            n = DIFF_HEADS * 2 * DIFF_HEAD_DIM
            w = b_w_in[j]
            qt, k, vt = _attn_projections(h, g[0], w[:, :n], w[:, n:2 * n], w[:, 2 * n:],
                                          DIFF_HEAD_DIM ** -0.5 * LOG2E)
            mixer_out = (_diff_attention(qt, k, vt, rel_bias, b_lam[j], b_sub_g[j], lambda_init),
                         b_w_out[j], g[1])
        elif m == 2:
            n = FOX_HEADS * FOX_HEAD_DIM
            w = c_w_in[j]
            qt, k, vt, kaug = _attn_projections(h, g[0], w[:, :n], w[:, n:2 * n], w[:, 2 * n:3 * n],
                                                FOX_HEAD_DIM ** -0.5 * LOG2E, fox=(w[:, 3 * n:], c_b_f[j]))
            mixer_out = (_fox_attention(qt, k, kaug, vt), c_w_out[j], g[1])
        else:
            h = _rglru_layer(h, g[0], d_w_in[j], d_conv_w[j], d_conv_b[j], d_w_r[j], d_b_r[j],
                             d_w_i[j], d_b_i[j], d_lam[j], d_w_out[j], g[1])
        h = _conv_ffn(h, g[2], ffn_w_up[layer], ffn_conv_w[layer], ffn_conv_b[layer],
                      ffn_w_down[layer], g[3], mixer_out)
    return h
```

```python
import functools
import math

import numpy as np
import jax
import jax.numpy as jnp
from jax import lax
from jax.experimental import pallas as pl
from jax.experimental.pallas import tpu as pltpu

F32 = jnp.float32
BF16 = jnp.bfloat16

D_MODEL = 1024
CHUNK = 64
N_MIXERS = 4
EPS = 1e-6
GMLP_CHUNK = 128
GMLP_WIDTH = 1024
GMLP_GROUPS = 8
DIFF_HEADS = 8
DIFF_HEAD_DIM = 64
FOX_HEADS = 16
FOX_HEAD_DIM = 64
RNN_WIDTH = 1280
RNN_BLOCKS = 16
RNN_BLOCK_DIM = RNN_WIDTH // RNN_BLOCKS
RNN_CONV = 4
RGLRU_C = 8.0
D_FF = 2816
FFN_CONV = 3
REL_BUCKETS = 32
REL_MAX_DIST = 128

LANES = 128
SUBLANES = 8
MXU_TILE = 256
MASKED = -1e30
TINY = 1e-30
VMEM_LIMIT = 56 * 1024 * 1024

FFN_TILE = 512
FFN_CHUNK = 256
FFN_SLOTS = 6
ROW_TILE = 512
GMLP_TILE = 512
RNN_TILE = 256
RNN_SLABS = 2
ATT_TILE = 512
SOFTMAX_GROUP = 256
SUM_ROWS = 16
DECAY_PIECES = 3
CUMSUM_BLOCK = 128
LOG2E = math.log2(math.e)

_NT = (((1,), (1,)), ((), ()))


def _params(*sem):
    return pltpu.CompilerParams(dimension_semantics=sem, vmem_limit_bytes=VMEM_LIMIT)


def _resident(shape):
    zeros = (0,) * len(shape)
    return pl.BlockSpec(shape, lambda *_: zeros, pipeline_mode=pl.Buffered(1))


def _rows(tm, width):
    return pl.BlockSpec((None, tm, width), lambda b, s: (b, s, 0))


def _cols(height, tm):
    return pl.BlockSpec((None, height, tm), lambda b, s: (b, 0, s))


def _rms(x, g):
    return x * lax.rsqrt(jnp.mean(x * x, axis=-1, keepdims=True) + EPS) * g


def _dot(a, b):
    return jnp.dot(a, b, preferred_element_type=F32)


def _ffn_kernel(*refs, tm, nc, fc, mixer_out):
    if mixer_out:
        mix_ref, wmix_ref, gmix_ref = refs[:3]
        refs = refs[3:]
    (h_ref, gin_ref, wup_ref, cw_ref, cb_ref, wdn_ref, gout_ref, o_ref,
     carry_ref, acc_ref, y_ref, *bufs) = refs
    ext_refs = tuple((bufs[2 * i], bufs[2 * i + 1]) for i in range(FFN_SLOTS))
    act_refs = bufs[2 * FFN_SLOTS:]

    @pl.when(pl.program_id(1) == 0)
    def _():
        carry_ref[...] = jnp.zeros_like(carry_ref)

    h = h_ref[...]
    if mixer_out:
        h = h + _rms(_dot(mix_ref[...], wmix_ref[...]), gmix_ref[...])
    y_ref[...] = _rms(h, gin_ref[...]).astype(BF16)

    def up(c, slot):
        for branch in range(2):
            idx = c + branch * nc
            ext = ext_refs[slot][branch]
            u = _dot(y_ref[...], wup_ref[:, idx * fc:(idx + 1) * fc])
            ext[0:SUBLANES, :] = carry_ref[idx]
            ext[SUBLANES:SUBLANES + tm, :] = u
            carry_ref[idx] = u[tm - SUBLANES:tm, :]

    def activate(c, slot):
        conv = []
        for branch in range(2):
            idx = c + branch * nc
            ext = ext_refs[slot][branch]
            w = cw_ref[:, idx * fc:(idx + 1) * fc]
            conv.append(ext[pl.ds(SUBLANES, tm), :] * w[2:3] + ext[pl.ds(SUBLANES - 1, tm), :] * w[1:2]
                        + ext[pl.ds(SUBLANES - 2, tm), :] * w[0:1] + cb_ref[:, idx * fc:(idx + 1) * fc])
        act_refs[slot][...] = (jax.nn.gelu(conv[0]) * conv[1]).astype(BF16)

    def down(c, slot):
        part = _dot(act_refs[slot][...], wdn_ref[c * fc:(c + 1) * fc, :])
        if c == 0:
            acc_ref[...] = part
        else:
            acc_ref[...] += part

    lag = FFN_SLOTS - 1
    for c in range(min(lag, nc)):
        up(c, c % FFN_SLOTS)
    for c in range(nc + lag):
        if lag <= c:
            down(c - lag, (c - lag) % FFN_SLOTS)
        if c + lag < nc:
            up(c + lag, (c + lag) % FFN_SLOTS)
        if c < nc:
            activate(c, c % FFN_SLOTS)
    o_ref[...] = h + _rms(acc_ref[...], gout_ref[...])


def _conv_ffn(h, g_in, w_up, conv_w, conv_b, w_down, g_out, mixer_out=None):
    B, S, D = h.shape
    tm, fc = min(FFN_TILE, S), FFN_CHUNK
    nc = D_FF // fc
    assert D_FF % fc == 0 and S % tm == 0
    wup, cw, cb, wdn = w_up.astype(BF16), conv_w, conv_b.reshape(1, -1), w_down.astype(BF16)
    mix_specs, mix_args = [], []
    if mixer_out is not None:
        mix, w_mix, g_mix = mixer_out
        n = mix.shape[-1]
        mix_specs = [_rows(tm, n), _resident((n, D)), _resident((1, D))]
        mix_args = [mix, w_mix.astype(BF16), g_mix.reshape(1, D)]
    return pl.pallas_call(
        functools.partial(_ffn_kernel, tm=tm, nc=nc, fc=fc, mixer_out=mixer_out is not None),
        out_shape=jax.ShapeDtypeStruct((B, S, D), F32),
        grid=(B, S // tm),
        in_specs=mix_specs + [_rows(tm, D), _resident((1, D)), _resident(wup.shape), _resident(cw.shape),
                              _resident(cb.shape), _resident(wdn.shape), _resident((1, D))],
        out_specs=_rows(tm, D),
        scratch_shapes=[pltpu.VMEM((2 * nc, SUBLANES, fc), F32),
                        pltpu.VMEM((tm, D), F32),
                        pltpu.VMEM((tm, D), BF16)]
        + [pltpu.VMEM((SUBLANES + tm, fc), F32)] * (2 * FFN_SLOTS)
        + [pltpu.VMEM((tm, fc), BF16)] * FFN_SLOTS,
        compiler_params=_params("parallel", "arbitrary"),
        name="conv_ffn",
    )(*mix_args, h, g_in.reshape(1, D), wup, cw, cb, wdn, g_out.reshape(1, D))


def _gmlp_kernel(h_ref, gin_ref, win_ref, lng_ref, lnb_ref, ws_ref, bs_ref, wout_ref, gout_ref,
                 o_ref, vo_ref, *, tm):
    W, G, C = GMLP_WIDTH, GMLP_GROUPS, GMLP_CHUNK
    h = h_ref[...]
    y = _rms(h, gin_ref[...]).astype(BF16)
    v_pre = _dot(y, win_ref[:, W:2 * W])
    u_pre = _dot(y, win_ref[:, 0:W])
    v = jax.nn.gelu(v_pre)
    mu = jnp.mean(v, axis=-1, keepdims=True)
    vc = v - mu
    var = jnp.mean(vc * vc, axis=-1, keepdims=True)
    vb = (vc * lax.rsqrt(var + EPS) * lng_ref[...] + lnb_ref[...]).astype(BF16)
    p_out = lax.broadcasted_iota(jnp.int32, (C, C), 0)
    q_in = lax.broadcasted_iota(jnp.int32, (C, C), 1)
    keep = (q_in // CHUNK) <= (p_out // CHUNK)
    for g in range(G):
        wsg = jnp.where(keep, ws_ref[g], 0.0).astype(BF16)
        for n in range(tm // C):
            blk = _dot(wsg, vb[n * C:(n + 1) * C, g * C:(g + 1) * C])
            vo_ref[n * C:(n + 1) * C, g * C:(g + 1) * C] = blk + bs_ref[:, g * C:(g + 1) * C]
    out = _dot((jax.nn.gelu(u_pre) * vo_ref[...]).astype(BF16), wout_ref[...])
    o_ref[...] = h + _rms(out, gout_ref[...])


def _gmlp_layer(h, g_in, w_in, ln_g, ln_b, w_s, b_s, w_out, g_out):
    B, S, D = h.shape
    tm = min(GMLP_TILE, S)
    assert S % tm == 0 and tm % GMLP_CHUNK == 0
    W = GMLP_WIDTH
    bs = jnp.repeat(b_s.T, GMLP_CHUNK, axis=1)
    return pl.pallas_call(
        functools.partial(_gmlp_kernel, tm=tm),
        out_shape=jax.ShapeDtypeStruct((B, S, D), F32),
        grid=(B, S // tm),
        in_specs=[_rows(tm, D), _resident((1, D)), _resident((D, 2 * W)), _resident((1, W)),
                  _resident((1, W)), _resident(w_s.shape), _resident(bs.shape),
                  _resident((W, D)), _resident((1, D))],
        out_specs=_rows(tm, D),
        scratch_shapes=[pltpu.VMEM((tm, W), F32)],
        compiler_params=_params("parallel", "parallel"),
        name="gmlp_mixer",
    )(h, g_in.reshape(1, D), w_in.astype(BF16), ln_g.reshape(1, W), ln_b.reshape(1, W), w_s, bs,
      w_out.astype(BF16), g_out.reshape(1, D))


def _project_qkv(h_ref, gin_ref, wqt_ref, wk_ref, wvt_ref, qt_ref, k_ref, vt_ref, scale):
    y = _rms(h_ref[...], gin_ref[...]).astype(BF16)
    qt = lax.dot_general(wqt_ref[...], y, _NT, preferred_element_type=F32)
    qt_ref[...] = (qt * scale).astype(BF16)
    k_ref[...] = _dot(y, wk_ref[...]).astype(BF16)
    vt_ref[...] = lax.dot_general(wvt_ref[...], y, _NT, preferred_element_type=F32).astype(BF16)
    return y


def _qkv_kernel(h_ref, gin_ref, wqt_ref, wk_ref, wvt_ref, qt_ref, k_ref, vt_ref, *, scale):
    _project_qkv(h_ref, gin_ref, wqt_ref, wk_ref, wvt_ref, qt_ref, k_ref, vt_ref, scale)


def _fox_proj_kernel(h_ref, gin_ref, wqt_ref, wk_ref, wvt_ref, wf_ref, bf_ref, place_ref,
                     qt_ref, k_ref, vt_ref, kaug_ref, carry_ref, *, scale, tm):
    @pl.when(pl.program_id(1) == 0)
    def _():
        carry_ref[...] = jnp.zeros_like(carry_ref)

    y = _project_qkv(h_ref, gin_ref, wqt_ref, wk_ref, wvt_ref, qt_ref, k_ref, vt_ref, scale)
    f = _dot(y, wf_ref[...]) + bf_ref[...]
    logf = jnp.minimum(f, 0.0) - jnp.log(1.0 + jnp.exp(-jnp.abs(f)))
    n = CUMSUM_BLOCK
    row = lax.broadcasted_iota(jnp.int32, (n, n), 0)
    col = lax.broadcasted_iota(jnp.int32, (n, n), 1)
    tri = jnp.where(col <= row, 1.0, 0.0).astype(BF16)
    total = carry_ref[0:1, :]
    for r0 in range(0, tm, n):
        block = _dot_pieces(tri, logf[r0:r0 + n, :]) + total
        kaug_ref[r0:r0 + n, :] = _dot_pieces(block * (-LOG2E), place_ref, left=True).astype(BF16)
        total = block[n - 1:n, :]
    carry_ref[...] = jnp.broadcast_to(total, carry_ref.shape)


def _dot_pieces(a, b, left=False):
    rest = a if left else b
    out = None
    for c in range(DECAY_PIECES):
        piece = rest.astype(BF16)
        rest = rest - piece.astype(F32)
        term = _dot(piece, b[c]) if left else _dot(a, piece)
        out = term if out is None else out + term
    return out


def _decay_placement(n_heads):
    assert DECAY_PIECES * n_heads <= LANES
    place = np.zeros((DECAY_PIECES, LANES, LANES), np.float32)
    for h in range(n_heads):
        for c in range(DECAY_PIECES):
            place[c, h, DECAY_PIECES * h + c] = 1.0
    return jnp.asarray(place, BF16)


def _attn_projections(h, g_in, wq, wk, wv, scale, fox=None):
    B, S, D = h.shape
    tm = min(ROW_TILE, S)
    assert S % tm == 0
    nq, nk, nv = wq.shape[1], wk.shape[1], wv.shape[1]
    in_specs = [_rows(tm, D), _resident((1, D)), _resident((nq, D)), _resident((D, nk)),
                _resident((nv, D))]
    out_shape = [jax.ShapeDtypeStruct((B, nq, S), BF16), jax.ShapeDtypeStruct((B, S, nk), BF16),
                 jax.ShapeDtypeStruct((B, nv, S), BF16)]
    out_specs = [_cols(nq, tm), _rows(tm, nk), _cols(nv, tm)]
    args = [h, g_in.reshape(1, D), wq.T.astype(BF16), wk.astype(BF16), wv.T.astype(BF16)]
    if fox is None:
        return pl.pallas_call(
            functools.partial(_qkv_kernel, scale=scale),
            out_shape=out_shape, grid=(B, S // tm), in_specs=in_specs, out_specs=out_specs,
            compiler_params=_params("parallel", "parallel"), name="qkv_proj",
        )(*args)
    wf, bf = fox
    H = wf.shape[1]
    wf_pad = jnp.zeros((D, LANES), F32).at[:, :H].set(wf).astype(BF16)
    bf_pad = jnp.zeros((1, LANES), F32).at[0, :H].set(bf)
    place = _decay_placement(H)
    in_specs += [_resident((D, LANES)), _resident((1, LANES)), _resident(place.shape)]
    out_shape += [jax.ShapeDtypeStruct((B, S, place.shape[2]), BF16)]
    out_specs += [_rows(tm, place.shape[2])]
    return pl.pallas_call(
        functools.partial(_fox_proj_kernel, scale=scale, tm=tm),
        out_shape=out_shape, grid=(B, S // tm), in_specs=in_specs, out_specs=out_specs,
        scratch_shapes=[pltpu.VMEM((SUBLANES, LANES), F32)],
        compiler_params=_params("parallel", "arbitrary"), name="fox_proj",
    )(*args, wf_pad, bf_pad, place)


def _softmax_reset(m_ref, acc_ref):
    m_ref[...] = jnp.full(m_ref.shape, MASKED, F32)
    acc_ref[...] = jnp.zeros_like(acc_ref)


def _consume_tile(vt, s_refs, top_refs, m_refs, acc_refs, n_cols, refill=None):
    group = n_cols // len(s_refs)
    pending = None
    ones = jnp.ones((SUM_ROWS, vt.shape[1]), BF16)

    def accumulate(g, alpha, p):
        rows = acc_refs[g].shape[0] - SUM_ROWS
        r0 = 0 if rows == vt.shape[0] else (2 * g * group // n_cols) * rows
        values = jnp.concatenate([vt[r0:r0 + rows, :], ones], axis=0)
        acc_refs[g][...] = alpha * acc_refs[g][...] + _dot(values, p)

    for g in range(len(s_refs)):
        if refill is not None:
            refill(g)
        m_old = m_refs[g][...]
        m_new = jnp.maximum(m_old, top_refs[g][...])
        alpha = jnp.exp2(m_old - m_new)
        p = jnp.exp2(s_refs[g][...] - m_new).astype(BF16)
        m_refs[g][...] = m_new
        if pending is not None:
            accumulate(*pending)
        pending = (g, alpha, p)
    accumulate(*pending)


def _attention_blocks(nq, prepare, keys_of, vt_of, qa_refs, s_slots, top_slots, m_sets, acc_sets,
                      adjust_of, finalize):
    blocks = [(qi, j) for qi in range(nq) for j in range(qi + 1)]
    n_cols = qa_refs[0].shape[1]
    group = n_cols // len(s_slots[0])

    def score(g, qi, j, slot):
        s = _dot(keys_of(j), qa_refs[qi % 2][:, g * group:(g + 1) * group])
        adjust = adjust_of(qi, j)
        if adjust is not None:
            s = adjust(s, g * group)
        s_slots[slot][g][...] = s
        top_slots[slot][g][...] = jnp.max(s, axis=0, keepdims=True)

    prepare(0)
    for g in range(len(s_slots[0])):
        score(g, 0, 0, 0)
    for b, (qi, j) in enumerate(blocks):
        m_refs, acc_refs = m_sets[qi % 2], acc_sets[qi % 2]
        if j == 0:
            for g in range(len(m_refs)):
                _softmax_reset(m_refs[g], acc_refs[g])
        refill = None
        if b + 1 < len(blocks):
            nqi, nj = blocks[b + 1]
            if nqi != qi:
                prepare(nqi)

            def refill(g, nqi=nqi, nj=nj, slot=(b + 1) % 2):
                score(g, nqi, nj, slot)
        _consume_tile(vt_of(j), s_slots[b % 2], top_slots[b % 2], m_refs, acc_refs, n_cols, refill)
        if j == qi:
            finalize(qi, acc_refs)


def _attention_scratch(t, query_rows, value_dim):
    groups = 2 * t // SOFTMAX_GROUP
    shapes = ((t, SOFTMAX_GROUP), (1, SOFTMAX_GROUP), (1, SOFTMAX_GROUP),
              (value_dim + SUM_ROWS, SOFTMAX_GROUP))
    return ([pltpu.VMEM((query_rows, 2 * t), BF16)] * 2
            + [pltpu.VMEM(shape, F32) for shape in shapes for _ in range(2 * groups)])


def _split_scratch(scratch, t):
    G = 2 * t // SOFTMAX_GROUP
    qa_refs, rest = scratch[0:2], scratch[2:]
    kinds = [(rest[k * 2 * G:k * 2 * G + G], rest[k * 2 * G + G:(k + 1) * 2 * G]) for k in range(4)]
    return (qa_refs, *kinds)


def _split_rows(qt_ref, qa_ref, d, t):
    qt = qt_ref[...]
    row = lax.broadcasted_iota(jnp.int32, qt.shape, 0)
    zero = jnp.zeros_like(qt)
    qa_ref[0:2 * d, 0:t] = jnp.where(row < d, qt, zero)
    qa_ref[0:2 * d, t:2 * t] = jnp.where(row >= d, qt, zero)


def _t5_bucket(rel):
    half = REL_BUCKETS // 2
    max_exact = half // 2
    n = jnp.abs(rel)
    ret = jnp.where(rel > 0, half, 0)
    nf = jnp.maximum(n, 1).astype(jnp.float32)
    large = max_exact + (jnp.log(nf / max_exact) / math.log(REL_MAX_DIST / max_exact)
                         * (half - max_exact)).astype(jnp.int32)
    large = jnp.minimum(large, half - 1)
    return ret + jnp.where(n < max_exact, n, large)


def _diff_bias_tiles(rel_bias, t):
    assert t >= REL_MAX_DIST
    kpos = jnp.arange(t)[:, None]
    qpos = jnp.arange(t)[None, :]
    rel = jnp.stack([kpos - qpos - t, kpos - qpos], axis=0)
    bucket = _t5_bucket(rel)
    far = rel_bias[_t5_bucket(jnp.int32(-REL_MAX_DIST))]
    bias = jnp.zeros((rel_bias.shape[1],) + bucket.shape, F32)
    for b in range(REL_BUCKETS):
        bias = jnp.where(bucket == b, (rel_bias[b] - far)[:, None, None, None], bias)
    keep = jnp.stack([jnp.ones((t, t), bool), (kpos // CHUNK) <= (qpos // CHUNK)], axis=0)
    return jnp.where(keep, bias * LOG2E, MASKED)


def _diff_attn_kernel(qt_ref, k_ref, vt_ref, bias_ref, lam_ref, subg_ref, o_ref, *scratch,
                      t, nq, lambda_init):
    W = 2 * DIFF_HEAD_DIM
    qa_refs, s_slots, top_slots, m_sets, acc_sets = _split_scratch(scratch, t)
    lam = lam_ref[...]
    lam_full = (jnp.exp(jnp.sum(lam[0:1] * lam[1:2], axis=-1, keepdims=True))
                - jnp.exp(jnp.sum(lam[2:3] * lam[3:4], axis=-1, keepdims=True)) + lambda_init)

    def prepare(qi):
        _split_rows(qt_ref.at[:, qi * t:(qi + 1) * t], qa_refs[qi % 2], DIFF_HEAD_DIM, t)

    def add_bias(tile):
        def adjust(s, c0):
            return s + bias_ref[tile, :, c0 % t:c0 % t + s.shape[1]]
        return adjust

    def adjust_of(qi, j):
        return add_bias(1) if j == qi else add_bias(0) if j == qi - 1 else None

    def finalize(qi, acc_refs):
        o = jnp.concatenate([acc[0:W, :] / acc[W:W + 1, :] for acc in acc_refs], axis=1)
        o = (o[:, 0:t] - lam_full * o[:, t:2 * t]).T
        o_ref[qi * t:(qi + 1) * t, :] = (_rms(o, subg_ref[...]) * (1.0 - lambda_init)).astype(BF16)

    _attention_blocks(nq, prepare, lambda j: k_ref[j * t:(j + 1) * t, :],
                      lambda j: vt_ref[:, j * t:(j + 1) * t], qa_refs, s_slots, top_slots, m_sets,
                      acc_sets, adjust_of, finalize)


def _diff_attention(qt, k, vt, rel_bias, lam, sub_g, lambda_init):
    B, S, _ = k.shape
    t = min(ATT_TILE, S)
    assert S % t == 0
    H, W = DIFF_HEADS, 2 * DIFF_HEAD_DIM
    bias = _diff_bias_tiles(rel_bias, t)
    return pl.pallas_call(
        functools.partial(_diff_attn_kernel, t=t, nq=S // t, lambda_init=lambda_init),
        out_shape=jax.ShapeDtypeStruct((B, S, H * W), BF16),
        grid=(B, H),
        in_specs=[pl.BlockSpec((None, W, S), lambda b, h: (b, h, 0)),
                  pl.BlockSpec((None, S, W), lambda b, h: (b, 0, h)),
                  pl.BlockSpec((None, W, S), lambda b, h: (b, h, 0)),
                  pl.BlockSpec((None, 2, t, t), lambda b, h: (h, 0, 0, 0)),
                  pl.BlockSpec((4, DIFF_HEAD_DIM), lambda b, h: (0, 0)),
                  pl.BlockSpec((1, W), lambda b, h: (0, 0))],
        out_specs=pl.BlockSpec((None, S, W), lambda b, h: (b, 0, h)),
        scratch_shapes=_attention_scratch(t, W, W),
        compiler_params=_params("parallel", "parallel"), name="diff_attention",
    )(qt, k, vt, bias, lam, sub_g.reshape(1, W))


def _fox_attn_kernel(qt_ref, k_ref, kaug_ref, vt_ref, o_ref, *scratch, t, nq):
    d = FOX_HEAD_DIM
    qa_refs, s_slots, top_slots, m_sets, acc_sets = _split_scratch(scratch, t)
    r = lax.broadcasted_iota(jnp.int32, (LANES, 2 * t), 0)
    c = lax.broadcasted_iota(jnp.int32, (LANES, 2 * t), 1)
    first = DECAY_PIECES * (2 * pl.program_id(1) + jnp.where(c < t, 0, 1))
    pick = jnp.where(r < first, 0.0, jnp.where(r < first + DECAY_PIECES, 1.0, 0.0)).astype(BF16)
    for qa_ref in qa_refs:
        qa_ref[2 * d:, :] = pick

    def prepare(qi):
        _split_rows(qt_ref.at[:, qi * t:(qi + 1) * t], qa_refs[qi % 2], d, t)

    def keys_of(j):
        return jnp.concatenate([k_ref[j * t:(j + 1) * t, :], kaug_ref[j * t:(j + 1) * t, :]], axis=1)

    def causal(s, c0):
        key = lax.broadcasted_iota(jnp.int32, s.shape, 0)
        qry = (lax.broadcasted_iota(jnp.int32, s.shape, 1) + c0) & (t - 1)
        return jnp.where(key <= qry, s, MASKED)

    def finalize(qi, acc_refs):
        o = jnp.concatenate([acc[0:d, :] / acc[d:d + 1, :] for acc in acc_refs], axis=1)
        o = jnp.concatenate([o[:, 0:t], o[:, t:2 * t]], axis=0)
        o_ref[qi * t:(qi + 1) * t, :] = o.T.astype(BF16)

    _attention_blocks(nq, prepare, keys_of, lambda j: vt_ref[:, j * t:(j + 1) * t], qa_refs, s_slots,
                      top_slots, m_sets, acc_sets, lambda qi, j: causal if j == qi else None, finalize)


def _fox_attention(qt, k, kaug, vt):
    B, S, _ = k.shape
    t = min(ATT_TILE, S)
    assert S % t == 0 and (t & (t - 1)) == 0
    H, d = FOX_HEADS, FOX_HEAD_DIM
    W = 2 * d
    return pl.pallas_call(
        functools.partial(_fox_attn_kernel, t=t, nq=S // t),
        out_shape=jax.ShapeDtypeStruct((B, S, H * d), BF16),
        grid=(B, H // 2),
        in_specs=[pl.BlockSpec((None, W, S), lambda b, p: (b, p, 0)),
                  pl.BlockSpec((None, S, W), lambda b, p: (b, 0, p)),
                  pl.BlockSpec((None, S, LANES), lambda b, p: (b, 0, 0)),
                  pl.BlockSpec((None, W, S), lambda b, p: (b, p, 0))],
        out_specs=pl.BlockSpec((None, S, W), lambda b, p: (b, 0, p)),
        scratch_shapes=_attention_scratch(t, W + LANES, d),
        compiler_params=_params("parallel", "parallel"), name="fox_attention",
    )(qt, k, kaug, vt)


def _block_diag_dot(x, w_ref, lo):
    n, bd, tile = x.shape[1], RNN_BLOCK_DIM, MXU_TILE
    assert lo % bd == 0 and n % bd == 0
    outs = []
    for c0 in range(lo, lo + n, tile):
        c1 = min(c0 + tile, lo + n)
        k0 = max((c0 // bd) * bd // LANES * LANES, lo)
        k1 = min(-(-(-(-c1 // bd) * bd) // LANES) * LANES, lo + n)
        outs.append(_dot(x[:, k0 - lo:k1 - lo], w_ref[k0:k1, c0:c1]))
    return jnp.concatenate(outs, axis=1)


def _rglru_kernel(h_ref, gin_ref, win_ref, cw_ref, cb_ref, wr_ref, br_ref, wi_ref, bi_ref, lam_ref,
                  wout_ref, gout_ref, o_ref, ext_ref, xcarry_ref, hcarry_ref, hs_ref, *, tm):
    R = RNN_WIDTH
    Rs = R // RNN_SLABS

    @pl.when(pl.program_id(1) == 0)
    def _():
        xcarry_ref[...] = jnp.zeros_like(xcarry_ref)
        hcarry_ref[...] = jnp.zeros_like(hcarry_ref)

    h_in = h_ref[...]
    y = _rms(h_in, gin_ref[...]).astype(BF16)
    row = lax.broadcasted_iota(jnp.int32, (tm, Rs), 0) % SUBLANES
    out = None
    for lo in range(0, R, Rs):
        cols = slice(lo, lo + Rs)
        x = _dot(y, win_ref[:, R + lo:R + lo + Rs])
        ext_ref[0:SUBLANES, cols] = xcarry_ref[:, cols]
        ext_ref[SUBLANES:SUBLANES + tm, cols] = x
        xcarry_ref[:, cols] = x[tm - SUBLANES:tm, :]
        cw = cw_ref[:, cols]
        xr = x * cw[3:4] + cb_ref[:, cols]
        for k in range(1, RNN_CONV):
            xr = xr + ext_ref[pl.ds(SUBLANES - k, tm), cols] * cw[RNN_CONV - 1 - k:RNN_CONV - k]
        xb = xr.astype(BF16)
        r = jax.nn.sigmoid(_block_diag_dot(xb, wr_ref, lo) + br_ref[:, cols])
        i = jax.nn.sigmoid(_block_diag_dot(xb, wi_ref, lo) + bi_ref[:, cols])
        lam = lam_ref[:, cols]
        softplus_neg = jnp.maximum(-lam, 0.0) + jnp.log(1.0 + jnp.exp(-jnp.abs(lam)))
        a = jnp.exp((-RGLRU_C) * r * softplus_neg)
        one_minus_a2 = 1.0 - a * a
        u = one_minus_a2 * lax.rsqrt(jnp.maximum(one_minus_a2, TINY)) * (i * xr)

        shift = 1
        while shift < SUBLANES:
            live = row >= shift
            a_prev = jnp.where(live, pltpu.roll(a, shift, axis=0), 1.0)
            u_prev = jnp.where(live, pltpu.roll(u, shift, axis=0), 0.0)
            u = a * u_prev + u
            a = a * a_prev
            shift *= 2
        carry = hcarry_ref[:, cols]
        for g in range(tm // SUBLANES):
            rows = slice(g * SUBLANES, (g + 1) * SUBLANES)
            hg = a[rows] * carry + u[rows]
            hs_ref[rows, cols] = hg
            carry = jnp.broadcast_to(hg[SUBLANES - 1:SUBLANES, :], (SUBLANES, Rs))
        hcarry_ref[:, cols] = carry

        gate = jax.nn.gelu(_dot(y, win_ref[:, cols]))
        part = _dot((hs_ref[:, cols] * gate).astype(BF16), wout_ref[cols, :])
        out = part if out is None else out + part
    o_ref[...] = h_in + _rms(out, gout_ref[...])


def _block_diag(w):
    n, c, d = w.shape
    eye = jnp.eye(n, dtype=w.dtype)
    return (w[:, :, None, :] * eye[:, None, :, None]).reshape(n * c, n * d)


def _rglru_layer(h, g_in, w_in, conv_w, conv_b, w_r, b_r, w_i, b_i, lam, w_out, g_out):
    B, S, D = h.shape
    tm = min(RNN_TILE, S)
    assert S % tm == 0
    R = RNN_WIDTH
    vec = lambda a: a.reshape(1, -1)
    return pl.pallas_call(
        functools.partial(_rglru_kernel, tm=tm),
        out_shape=jax.ShapeDtypeStruct((B, S, D), F32),
        grid=(B, S // tm),
        in_specs=[_rows(tm, D), _resident((1, D)), _resident((D, 2 * R)), _resident((RNN_CONV, R)),
                  _resident((1, R)), _resident((R, R)), _resident((1, R)), _resident((R, R)),
                  _resident((1, R)), _resident((1, R)), _resident((R, D)), _resident((1, D))],
        out_specs=_rows(tm, D),
        scratch_shapes=[pltpu.VMEM((SUBLANES + tm, R), F32), pltpu.VMEM((SUBLANES, R), F32),
                        pltpu.VMEM((SUBLANES, R), F32), pltpu.VMEM((tm, R), F32)],
        compiler_params=_params("parallel", "arbitrary"), name="rglru_mixer",
    )(h, vec(g_in), w_in.astype(BF16), conv_w, vec(conv_b), _block_diag(w_r).astype(BF16), vec(b_r),
      _block_diag(w_i).astype(BF16), vec(b_i), vec(lam), w_out.astype(BF16), vec(g_out))


def kernel(x, norm_g, ffn_w_up, ffn_conv_w, ffn_conv_b, ffn_w_down, rel_bias, a_w_in, a_ln_g, a_ln_b, a_w_s, a_b_s, a_w_out, b_w_in, b_lam, b_sub_g, b_w_out, c_w_in, c_b_f, c_w_out, d_w_in, d_conv_w, d_conv_b, d_w_r, d_b_r, d_w_i, d_b_i, d_lam, d_w_out):
    depth = norm_g.shape[0]
    h = x
    for layer in range(depth):
        m = layer % N_MIXERS
        j = layer // N_MIXERS
        g = norm_g[layer]
        mixer_out = None
        if m == 0:
            h = _gmlp_layer(h, g[0], a_w_in[j], a_ln_g[j], a_ln_b[j], a_w_s[j], a_b_s[j], a_w_out[j], g[1])
        elif m == 1:
            lambda_init = 0.8 - 0.6 * math.exp(-0.3 * layer)
            n = DIFF_HEADS * 2 * DIFF_HEAD_DIM
            w = b_w_in[j]
            qt, k, vt = _attn_projections(h, g[0], w[:, :n], w[:, n:2 * n], w[:, 2 * n:],
                                          DIFF_HEAD_DIM ** -0.5 * LOG2E)
            mixer_out = (_diff_attention(qt, k, vt, rel_bias, b_lam[j], b_sub_g[j], lambda_init),
                         b_w_out[j], g[1])
        elif m == 2:
            n = FOX_HEADS * FOX_HEAD_DIM
            w = c_w_in[j]
            qt, k, vt, kaug = _attn_projections(h, g[0], w[:, :n], w[:, n:2 * n], w[:, 2 * n:3 * n],
                                                FOX_HEAD_DIM ** -0.5 * LOG2E, fox=(w[:, 3 * n:], c_b_f[j]))
            mixer_out = (_fox_attention(qt, k, kaug, vt), c_w_out[j], g[1])
        else:
            h = _rglru_layer(h, g[0], d_w_in[j], d_conv_w[j], d_conv_b[j], d_w_r[j], d_b_r[j],
                             d_w_i[j], d_b_i[j], d_lam[j], d_w_out[j], g[1])
        h = _conv_ffn(h, g[2], ffn_w_up[layer], ffn_conv_w[layer], ffn_conv_b[layer],
                      ffn_w_down[layer], g[3], mixer_out)
    return h
```
